```python
import jax
import jax.numpy as jnp
from jax import lax
import numpy as np

D_MODEL = 4096
BATCH = 32
SEQ = 256
DEPTH = 4
DEC_BATCH = 4
DEC_SEQ = 2048
PAST_LEN = 256

GRID_W = 64
N_MIXERS = 3
N_A = (DEPTH + 2) // 3
N_B = (DEPTH + 1) // 3
N_C = DEPTH // 3
D_FF = 11008
N_MOD = 9
EPS = 1e-6
A_HEADS = 8
A_DK = D_MODEL // (2 * A_HEADS)
A_DV = D_MODEL // A_HEADS
A_CHUNK = 64
A_PROJ = 2 * A_HEADS * A_DK + 2 * A_HEADS * A_DV + 4 * A_HEADS
B_DK = 128
B_HEADS = D_MODEL // B_DK
B_DV = D_MODEL // B_HEADS
B_PROJ = 3 * B_HEADS * B_DK + 2 * B_HEADS * B_DV
C_HEADS = 4
C_DK = D_MODEL // (2 * C_HEADS)
C_DV = D_MODEL // C_HEADS
C_RANK = 16
C_TAU = 16.0
C_PROJ = 2 * C_HEADS * C_DK + 2 * C_HEADS * C_DV + 2 * C_RANK
G_CHUNK = 16

kernel_name = 'hybrid_bidir_recurrent_diffusion_step'


def _rmsnorm(x, g):
    xf = x.astype(jnp.float32)
    y = xf * lax.rsqrt(jnp.mean(xf * xf, axis=-1, keepdims=True) + EPS)
    return y.astype(x.dtype) * g


def _modulate(h, shift, scale):
    return h * (1.0 + scale) + shift


def _swiglu(h, w_gate, w_up, w_down):
    return (jax.nn.silu(h @ w_gate) * (h @ w_up)) @ w_down


def _grid_dwconv(u, w, rows):
    b, t, ch = u.shape
    cols = t // rows
    g = jnp.pad(u.reshape(b, rows, cols, ch), ((0, 0), (1, 1), (1, 1), (0, 0)))
    out = sum(g[:, i:i + rows, j:j + cols, :] * w[i, j] for i in range(3) for j in range(3))
    return out.reshape(b, t, ch)


def _heads(a, n_heads):
    b, t, _ = a.shape
    return a.reshape(b, t, n_heads, -1).transpose(0, 2, 1, 3).astype(jnp.float32)


def _merge_heads_norm(o, g):
    o = o * lax.rsqrt(jnp.mean(o * o, axis=-1, keepdims=True) + EPS) * g.astype(jnp.float32)[:, None, :]
    b, h, t, d = o.shape
    return o.transpose(0, 2, 1, 3).reshape(b, t, h * d)


def _to_chunks(a, size):
    b, h, t = a.shape[:3]
    a = a.reshape((b, h, t // size, size) + a.shape[3:])
    return jnp.moveaxis(a, 2, 0)


def _from_chunks(a):
    a = jnp.moveaxis(a, 0, 2)
    return a.reshape((a.shape[0], a.shape[1], a.shape[2] * a.shape[3]) + a.shape[4:])


def _mlstm_scan(q, k, v, ig, lf, c0, n0, m0):
    mask = jnp.tril(jnp.ones((A_CHUNK, A_CHUNK), dtype=bool))

    def step(carry, inp):
        c_st, n_st, m_st = carry
        qc, kc, vc, ic, fc = inp
        b = jnp.cumsum(fc, axis=-1)
        dmat = jnp.where(mask, b[..., :, None] - b[..., None, :] + ic[..., None, :], -jnp.inf)
        m_inter = b + m_st[..., None]
        m_t = jnp.maximum(m_inter, jnp.max(dmat, axis=-1))
        s = jnp.einsum('bhtd,bhsd->bhts', qc, kc) * jnp.exp(dmat - m_t[..., None])
        e_inter = jnp.exp(m_inter - m_t)
        num = jnp.einsum('bhts,bhsv->bhtv', s, vc) + e_inter[..., None] * jnp.einsum('bhtd,bhdv->bhtv', qc, c_st)
        den = jnp.sum(s, axis=-1) + e_inter * jnp.einsum('bhtd,bhd->bht', qc, n_st)
        hc = num / jnp.maximum(jnp.abs(den), jnp.exp(-m_t))[..., None]
        b_end = b[..., -1]
        w = b_end[..., None] - b + ic
        m_new = jnp.maximum(b_end + m_st, jnp.max(w, axis=-1))
        carry_decay = jnp.exp(b_end + m_st - m_new)
        kw = kc * jnp.exp(w - m_new[..., None])[..., None]
        c_new = carry_decay[..., None, None] * c_st + jnp.einsum('bhsd,bhsv->bhdv', kw, vc)
        n_new = carry_decay[..., None] * n_st + jnp.sum(kw, axis=2)
        return (c_new, n_new, m_new), hc

    xs = tuple(_to_chunks(a, A_CHUNK) for a in (q, k, v, ig, lf))
    final, hs = lax.scan(step, (c0, n0, m0), xs)
    return _from_chunks(hs), final


def _gated_scan(q, k, v, lg, s0):
    mask = jnp.tril(jnp.ones((G_CHUNK, G_CHUNK), dtype=bool))[:, :, None]

    def step(s_st, inp):
        qc, kc, vc, gc = inp
        b = jnp.cumsum(gc, axis=2)
        o_inter = jnp.einsum('bhtd,bhdv->bhtv', qc * jnp.exp(b), s_st)
        decay = jnp.exp(jnp.where(mask, b[:, :, :, None, :] - b[:, :, None, :, :], -jnp.inf))
        a = jnp.einsum('bhtd,bhsd,bhtsd->bhts', qc, kc, decay)
        oc = o_inter + jnp.einsum('bhts,bhsv->bhtv', a, vc)
        b_end = b[:, :, -1:, :]
        s_new = jnp.exp(b_end[:, :, 0, :])[..., None] * s_st + jnp.einsum('bhsd,bhsv->bhdv', kc * jnp.exp(b_end - b), vc)
        return s_new, oc

    xs = tuple(_to_chunks(a, G_CHUNK) for a in (q, k, v, lg))
    final, outs = lax.scan(step, s0, xs)
    return _from_chunks(outs), (final,)


def _two_way(scan_fn, seq_fwd, seq_bwd, init):
    out_f, fin_f = scan_fn(*seq_fwd, *(s[:, 0] for s in init))
    out_b, fin_b = scan_fn(*(jnp.flip(a, axis=2) for a in seq_bwd), *(s[:, 1] for s in init))
    out = out_f + jnp.flip(out_b, axis=2)
    final = tuple(jnp.stack([f, g], axis=1) for f, g in zip(fin_f, fin_b))
    return out, final


def _mlstm_mixer(h, w_in, conv_w, gate_b, norm_g, w_out, state0, rows):
    b, t, _ = h.shape
    nq = A_HEADS * A_DK
    nv = A_HEADS * A_DV
    z = h @ w_in
    qk = jax.nn.silu(_grid_dwconv(z[..., :2 * nq], conv_w, rows))
    q = _heads(qk[..., :nq], A_HEADS) * (A_DK ** -0.5)
    k = _heads(qk[..., nq:], A_HEADS)
    v = _heads(z[..., 2 * nq:2 * nq + nv], A_HEADS)
    o_gate = jax.nn.sigmoid(z[..., 2 * nq + nv:2 * nq + 2 * nv].astype(jnp.float32))
    gates = z[..., 2 * nq + 2 * nv:].astype(jnp.float32).reshape(b, t, 2, 2, A_HEADS) + gate_b.astype(jnp.float32)
    gates = gates.transpose(2, 3, 0, 4, 1)
    seqs = [(q, k, v, gates[d, 0], jax.nn.log_sigmoid(gates[d, 1])) for d in range(2)]
    init = tuple(s.astype(jnp.float32) for s in state0)
    out, final = _two_way(_mlstm_scan, seqs[0], seqs[1], init)
    y = (o_gate * _merge_heads_norm(out, norm_g)).astype(h.dtype) @ w_out
    return y, final


def _hgrn2_mixer(h, w_in, conv_w, f_b, lb, norm_g, w_out, state0, rows):
    b, t, _ = h.shape
    nq = B_HEADS * B_DK
    ni = B_HEADS * B_DV
    z = h @ w_in
    qi = jax.nn.silu(_grid_dwconv(z[..., :nq + ni], conv_w, rows))
    q = _heads(qi[..., :nq], B_HEADS)
    inp = _heads(qi[..., nq:], B_HEADS)
    f_pre = z[..., nq + ni:3 * nq + ni].astype(jnp.float32).reshape(b, t, 2, nq) + f_b.astype(jnp.float32)
    out_gate = jax.nn.silu(z[..., 3 * nq + ni:].astype(jnp.float32))
    seqs = []
    for d in range(2):
        f = lb[d] + (1.0 - lb[d]) * jax.nn.sigmoid(f_pre[:, :, d])
        seqs.append((q, _heads(1.0 - f, B_HEADS), inp, _heads(jnp.log(f), B_HEADS)))
    out, final = _two_way(_gated_scan, seqs[0], seqs[1], (state0[0].astype(jnp.float32),))
    y = (out_gate * _merge_heads_norm(out, norm_g)).astype(h.dtype) @ w_out
    return y, final


def _gla_mixer(h, w_in, conv_w, w_a2, b_a, norm_g, w_out, state0, rows):
    b, t, _ = h.shape
    nq = C_HEADS * C_DK
    nv = C_HEADS * C_DV
    z = h @ w_in
    qk = jax.nn.silu(_grid_dwconv(z[..., :2 * nq], conv_w, rows))
    q = _heads(qk[..., :nq], C_HEADS) * (C_DK ** -0.5)
    k = _heads(qk[..., nq:], C_HEADS)
    v = _heads(z[..., 2 * nq:2 * nq + nv], C_HEADS)
    out_gate = jax.nn.silu(z[..., 2 * nq + nv:2 * nq + 2 * nv].astype(jnp.float32))
    a_low = z[..., 2 * nq + 2 * nv:].reshape(b, t, 2, C_RANK)
    seqs = []
    for d in range(2):
        lg = jax.nn.log_sigmoid((a_low[:, :, d] @ w_a2[d]).astype(jnp.float32) + b_a[d].astype(jnp.float32)) / C_TAU
        seqs.append((q, k, v, _heads(lg, C_HEADS)))
    out, final = _two_way(_gated_scan, seqs[0], seqs[1], (state0[0].astype(jnp.float32),))
    y = (out_gate * _merge_heads_norm(out, norm_g)).astype(h.dtype) @ w_out
    return y, final


def setup_inputs(seed: int = 0) -> dict:
    key = jax.random.key(seed)
    ks = iter(jax.random.split(key, 48))

    def nrm(shape, scale):
        return jax.random.normal(next(ks), shape, jnp.float32) * scale

    def gain(shape):
        return 1.0 + nrm(shape, 0.01)

    d = D_MODEL
    return {
        'x_prompt': nrm((BATCH, SEQ, d), 1.0),
        'x_sample': nrm((DEC_BATCH, DEC_SEQ, d), 1.0),
        'c': nrm((DEC_BATCH, d), 1.0),
        'c_ctx': nrm((d,), 1.0),
        'state_mlstm_C': nrm((DEC_BATCH, N_A, 2, A_HEADS, A_DK, A_DV), 0.5),
        'state_mlstm_n': nrm((DEC_BATCH, N_A, 2, A_HEADS, A_DK), 0.5),
        'state_mlstm_m': nrm((DEC_BATCH, N_A, 2, A_HEADS), 1.0),
        'state_hgrn_S': nrm((DEC_BATCH, N_B, 2, B_HEADS, B_DK, B_DV), 0.5),
        'state_gla_S': nrm((DEC_BATCH, N_C, 2, C_HEADS, C_DK, C_DV), 0.5),
        'mod_w': nrm((DEPTH, d, N_MOD * d), 0.5 * d ** -0.5),
        'mod_b': nrm((DEPTH, N_MOD * d), 0.01),
        'norm_g': gain((DEPTH, 3, d)),
        'ffn_w_gate': nrm((DEPTH, 2, d, D_FF), d ** -0.5),
        'ffn_w_up': nrm((DEPTH, 2, d, D_FF), d ** -0.5),
        'ffn_w_down': nrm((DEPTH, 2, D_FF, d), D_FF ** -0.5),
        'mlstm_w_in': nrm((N_A, d, A_PROJ), d ** -0.5),
        'mlstm_conv': nrm((N_A, 3, 3, 2 * A_HEADS * A_DK), 1.0 / 3.0),
        'mlstm_gate_b': nrm((N_A, 2, 2, A_HEADS), 0.1) + jnp.array([0.0, 3.0], jnp.float32)[:, None],
        'mlstm_norm_g': gain((N_A, A_HEADS, A_DV)),
        'mlstm_w_out': nrm((N_A, A_HEADS * A_DV, d), (A_HEADS * A_DV) ** -0.5),
        'hgrn_w_in': nrm((N_B, d, B_PROJ), d ** -0.5),
        'hgrn_conv': nrm((N_B, 3, 3, B_HEADS * (B_DK + B_DV)), 1.0 / 3.0),
        'hgrn_f_b': nrm((N_B, 2, B_HEADS * B_DK), 0.1),
        'hgrn_lb_logits': nrm((2, DEPTH, B_HEADS * B_DK), 0.5),
        'hgrn_norm_g': gain((N_B, B_HEADS, B_DV)),
        'hgrn_w_out': nrm((N_B, B_HEADS * B_DV, d), (B_HEADS * B_DV) ** -0.5),
        'gla_w_in': nrm((N_C, d, C_PROJ), d ** -0.5),
        'gla_conv': nrm((N_C, 3, 3, 2 * C_HEADS * C_DK), 1.0 / 3.0),
        'gla_w_a2': nrm((N_C, 2, C_RANK, C_HEADS * C_DK), C_RANK ** -0.5),
        'gla_b_a': nrm((N_C, 2, C_HEADS * C_DK), 0.1),
        'gla_norm_g': gain((N_C, C_HEADS, C_DV)),
        'gla_w_out': nrm((N_C, C_HEADS * C_DV, d), (C_HEADS * C_DV) ** -0.5),
        'final_norm_g': gain((d,)),
    }


def reference(x_prompt, x_sample, c, c_ctx, state_mlstm_C, state_mlstm_n, state_mlstm_m, state_hgrn_S,
              state_gla_S, mod_w, mod_b, norm_g, ffn_w_gate, ffn_w_up, ffn_w_down, mlstm_w_in, mlstm_conv,
              mlstm_gate_b, mlstm_norm_g, mlstm_w_out, hgrn_w_in, hgrn_conv, hgrn_f_b, hgrn_lb_logits,
              hgrn_norm_g, hgrn_w_out, gla_w_in, gla_conv, gla_w_a2, gla_b_a, gla_norm_g, gla_w_out,
              final_norm_g):
    lb_p = jax.nn.softmax(hgrn_lb_logits.astype(jnp.float32), axis=1)
    lb_all = jnp.cumsum(lb_p, axis=1) - lb_p[:, :1]

    def mix(l, h, st, rows):
        j = l // N_MIXERS
        kind = l % N_MIXERS
        if kind == 0:
            return _mlstm_mixer(h, mlstm_w_in[j], mlstm_conv[j], mlstm_gate_b[j], mlstm_norm_g[j],
                                mlstm_w_out[j], st, rows)
        if kind == 1:
            return _hgrn2_mixer(h, hgrn_w_in[j], hgrn_conv[j], hgrn_f_b[j], lb_all[:, l], hgrn_norm_g[j],
                                hgrn_w_out[j], st, rows)
        return _gla_mixer(h, gla_w_in[j], gla_conv[j], gla_w_a2[j], gla_b_a[j], gla_norm_g[j],
                          gla_w_out[j], st, rows)

    def trunk(x, cvec, init_state, rows):
        states = []
        for l in range(DEPTH):
            mod = jax.nn.silu(cvec) @ mod_w[l] + mod_b[l]
            sh1, sc1, g1, sh2, sc2, g2, sh3, sc3, g3 = (m[:, None, :] for m in jnp.split(mod, N_MOD, axis=-1))
            h = _modulate(_rmsnorm(x, norm_g[l, 0]), sh1, sc1)
            x = x + 0.5 * g1 * _swiglu(h, ffn_w_gate[l, 0], ffn_w_up[l, 0], ffn_w_down[l, 0])
            h = _modulate(_rmsnorm(x, norm_g[l, 1]), sh2, sc2)
            y, st = mix(l, h, init_state(l), rows)
            x = x + g2 * y
            h = _modulate(_rmsnorm(x, norm_g[l, 2]), sh3, sc3)
            x = x + 0.5 * g3 * _swiglu(h, ffn_w_gate[l, 1], ffn_w_up[l, 1], ffn_w_down[l, 1])
            states.append(st)
        return _rmsnorm(x, final_norm_g), states

    bp = x_prompt.shape[0]

    def ctx_init(l):
        kind = l % N_MIXERS
        if kind == 0:
            return (jnp.zeros((bp, 2, A_HEADS, A_DK, A_DV), jnp.float32),
                    jnp.zeros((bp, 2, A_HEADS, A_DK), jnp.float32),
                    jnp.zeros((bp, 2, A_HEADS), jnp.float32))
        if kind == 1:
            return (jnp.zeros((bp, 2, B_HEADS, B_DK, B_DV), jnp.float32),)
        return (jnp.zeros((bp, 2, C_HEADS, C_DK, C_DV), jnp.float32),)

    def cache_init(l):
        j = l // N_MIXERS
        kind = l % N_MIXERS
        if kind == 0:
            return (state_mlstm_C[:, j], state_mlstm_n[:, j], state_mlstm_m[:, j])
        if kind == 1:
            return (state_hgrn_S[:, j],)
        return (state_gla_S[:, j],)

    y_prompt, ctx_states = trunk(x_prompt, c_ctx[None, :], ctx_init, 1)
    y_sample, _ = trunk(x_sample, c, cache_init, x_sample.shape[1] // GRID_W)

    new_mlstm_C = jnp.stack([ctx_states[l][0] for l in range(0, DEPTH, N_MIXERS)], axis=1)
    new_mlstm_n = jnp.stack([ctx_states[l][1] for l in range(0, DEPTH, N_MIXERS)], axis=1)
    new_mlstm_m = jnp.stack([ctx_states[l][2] for l in range(0, DEPTH, N_MIXERS)], axis=1)
    new_hgrn_S = jnp.stack([ctx_states[l][0] for l in range(1, DEPTH, N_MIXERS)], axis=1)
    new_gla_S = jnp.stack([ctx_states[l][0] for l in range(2, DEPTH, N_MIXERS)], axis=1)
    return (y_prompt, y_sample, new_mlstm_C, new_mlstm_n, new_mlstm_m, new_hgrn_S, new_gla_S)
```

```python
import functools

import jax
import jax.numpy as jnp
from jax import lax
from jax.experimental import pallas as pl
from jax.experimental.pallas import tpu as pltpu

F32 = jnp.float32
BF16 = jnp.bfloat16

GRID_W = 64
EPS = 1e-6
GLA_TAU = 16.0
MLSTM_CHUNK = 64
GATED_CHUNK = 16
N_MOD = 9
COND_ROWS = 8

VMEM_LIMIT = 56 * 1024 * 1024
SCAN_TIME_BLOCK = 512
ROW_BLOCK = 256


def _params(*sem):
    return pltpu.CompilerParams(dimension_semantics=sem, vmem_limit_bytes=VMEM_LIMIT)


def _pick(n, pref, align):
    if n <= pref:
        return n
    best = None
    for cand in range(align, pref + 1, align):
        if n % cand == 0:
            best = cand
    assert best is not None, (n, pref, align)
    return best


def _sigmoid(x):
    return 1.0 / (1.0 + jnp.exp(-x))


def _silu(x):
    return x * _sigmoid(x)


def _log_sigmoid(x):
    return jnp.minimum(x, 0.0) - jnp.log1p(jnp.exp(-jnp.abs(x)))


class _CondRows:
    def __init__(self, n_prompt_rows, t_sample):
        assert n_prompt_rows % t_sample == 0
        self.n_prompt_rows, self.t_sample = n_prompt_rows, t_sample

    def block_rows(self, pref):
        return _pick(self.t_sample, pref, 8)

    def index(self, bm):
        assert self.t_sample % bm == 0

        def f(i):
            start = i * bm
            return jnp.where(start < self.n_prompt_rows, 0, 1 + (start - self.n_prompt_rows) // self.t_sample)
        return f


def _mod_body(c_ref, w_ref, b_ref, o_ref):
    a = _silu(c_ref[...]).astype(BF16)
    o_ref[0] = jnp.dot(a, w_ref[0].astype(BF16), preferred_element_type=F32) + b_ref[0]


def _modulation(cond, mod_w, mod_b):
    depth, d, n = mod_w.shape
    bn = _pick(n, 512, 128)
    return pl.pallas_call(
        _mod_body,
        out_shape=jax.ShapeDtypeStruct((depth, COND_ROWS, n), F32),
        grid=(depth, n // bn),
        in_specs=[pl.BlockSpec((COND_ROWS, d), lambda l, j: (0, 0)),
                  pl.BlockSpec((1, d, bn), lambda l, j: (l, 0, j)),
                  pl.BlockSpec((1, 1, bn), lambda l, j: (l, 0, j))],
        out_specs=pl.BlockSpec((1, COND_ROWS, bn), lambda l, j: (l, 0, j)),
        compiler_params=_params("arbitrary", "arbitrary"),
        name="modulation",
    )(cond, mod_w, mod_b.reshape(depth, 1, n))


def _norm_mod_body(sub, x_ref, g_ref, m_ref, o_ref):
    x = x_ref[...]
    y = x * lax.rsqrt(jnp.mean(x * x, axis=-1, keepdims=True) + EPS) * g_ref[...]
    m = m_ref[0]
    shift = m[3 * sub:3 * sub + 1, :]
    scale = m[3 * sub + 1:3 * sub + 2, :]
    o_ref[...] = (y * (1.0 + scale) + shift).astype(o_ref.dtype)


def _norm_mod(x, g, mod_l, sub, cond_of):
    n, d = x.shape
    bm = cond_of.block_rows(ROW_BLOCK)
    cr = cond_of.index(bm)
    return pl.pallas_call(
        functools.partial(_norm_mod_body, sub),
        out_shape=jax.ShapeDtypeStruct((n, d), BF16),
        grid=(n // bm,),
        in_specs=[pl.BlockSpec((bm, d), lambda i: (i, 0)),
                  pl.BlockSpec((1, d), lambda i: (0, 0)),
                  pl.BlockSpec((1, N_MOD, d), lambda i: (cr(i), 0, 0))],
        out_specs=pl.BlockSpec((bm, d), lambda i: (i, 0)),
        compiler_params=_params("parallel"),
        name="norm_mod",
    )(x, g.reshape(1, d), mod_l)


def _final_norm_body(x_ref, g_ref, o_ref):
    x = x_ref[...]
    o_ref[...] = x * lax.rsqrt(jnp.mean(x * x, axis=-1, keepdims=True) + EPS) * g_ref[...]


def _final_norm(x, g):
    n, d = x.shape
    bm = _pick(n, ROW_BLOCK, 8)
    return pl.pallas_call(
        _final_norm_body,
        out_shape=jax.ShapeDtypeStruct((n, d), F32),
        grid=(n // bm,),
        in_specs=[pl.BlockSpec((bm, d), lambda i: (i, 0)), pl.BlockSpec((1, d), lambda i: (0, 0))],
        out_specs=pl.BlockSpec((bm, d), lambda i: (i, 0)),
        compiler_params=_params("parallel"),
        name="final_norm",
    )(x, g.reshape(1, d))


def _matmul_body(x_ref, w_ref, o_ref):
    o_ref[...] = jnp.dot(x_ref[...], w_ref[...], preferred_element_type=F32).astype(o_ref.dtype)


def _matmul(x, w, out_dtype, bm_pref=1024, bn_pref=512):
    n, k = x.shape
    p = w.shape[1]
    bm = _pick(n, bm_pref, 8)
    bn = _pick(p, bn_pref, 128)
    return pl.pallas_call(
        _matmul_body,
        out_shape=jax.ShapeDtypeStruct((n, p), out_dtype),
        grid=(n // bm, p // bn),
        in_specs=[pl.BlockSpec((bm, k), lambda i, j: (i, 0)),
                  pl.BlockSpec((k, bn), lambda i, j: (0, j))],
        out_specs=pl.BlockSpec((bm, bn), lambda i, j: (i, j)),
        compiler_params=_params("parallel", "arbitrary"),
        name="matmul",
    )(x, w)


def _swiglu_up_body(h_ref, wg_ref, wu_ref, o_ref):
    h = h_ref[...]
    g = jnp.dot(h, wg_ref[...], preferred_element_type=F32)
    u = jnp.dot(h, wu_ref[...], preferred_element_type=F32)
    o_ref[...] = (_silu(g) * u).astype(o_ref.dtype)


def _swiglu_up(h, wg, wu, bm_pref=1024, bn_pref=256):
    n, k = h.shape
    f = wg.shape[1]
    bm = _pick(n, bm_pref, 8)
    bn = _pick(f, bn_pref, 128)
    return pl.pallas_call(
        _swiglu_up_body,
        out_shape=jax.ShapeDtypeStruct((n, f), BF16),
        grid=(n // bm, f // bn),
        in_specs=[pl.BlockSpec((bm, k), lambda i, j: (i, 0)),
                  pl.BlockSpec((k, bn), lambda i, j: (0, j)),
                  pl.BlockSpec((k, bn), lambda i, j: (0, j))],
        out_specs=pl.BlockSpec((bm, bn), lambda i, j: (i, j)),
        compiler_params=_params("parallel", "arbitrary"),
        name="swiglu_up",
    )(h, wg, wu)


def _matmul_residual_body(gate_row, coef, a_ref, w_ref, x_ref, m_ref, o_ref):
    y = jnp.dot(a_ref[...], w_ref[...], preferred_element_type=F32)
    gate = m_ref[0][gate_row:gate_row + 1, :]
    o_ref[...] = x_ref[...] + (coef * gate) * y


def _matmul_residual(a, w, x, mod_l, gate_row, coef, cond_of, bm_pref=512, bn_pref=512):
    n, k = a.shape
    d = w.shape[1]
    bm = cond_of.block_rows(bm_pref)
    bn = _pick(d, bn_pref, 128)
    cr = cond_of.index(bm)
    return pl.pallas_call(
        functools.partial(_matmul_residual_body, gate_row, coef),
        out_shape=jax.ShapeDtypeStruct((n, d), F32),
        grid=(n // bm, d // bn),
        in_specs=[pl.BlockSpec((bm, k), lambda i, j: (i, 0)),
                  pl.BlockSpec((k, bn), lambda i, j: (0, j)),
                  pl.BlockSpec((bm, bn), lambda i, j: (i, j)),
                  pl.BlockSpec((1, N_MOD, bn), lambda i, j: (cr(i), 0, j))],
        out_specs=pl.BlockSpec((bm, bn), lambda i, j: (i, j)),
        compiler_params=_params("parallel", "arbitrary"),
        name="matmul_residual",
    )(a, w, x, mod_l)


def _conv_body(rows, cols, z_ref, w_ref, o_ref):
    u = z_ref[...]
    t_len = u.shape[0]
    t = lax.broadcasted_iota(jnp.int32, (t_len, 1), 0)
    if rows == 1:
        r, c = jnp.zeros_like(t), t
    else:
        r, c = t >> (cols.bit_length() - 1), t & (cols - 1)
    w = w_ref[...]
    acc = jnp.zeros_like(u)
    for i in range(3):
        if rows == 1 and i != 1:
            continue
        for j in range(3):
            off = (i - 1) * cols + (j - 1)
            shifted = u if off == 0 else pltpu.roll(u, (-off) % t_len, 0)
            ok = (r + (i - 1) >= 0) & (r + (i - 1) < rows) & (c + (j - 1) >= 0) & (c + (j - 1) < cols)
            acc = acc + jnp.where(ok, shifted, 0.0) * w[3 * i + j:3 * i + j + 1, :]
    o_ref[...] = _silu(acc)


def _conv_silu(z, conv_w, n_conv, row0, batch, t_len, rows):
    cols = t_len // rows
    assert rows == 1 or (cols & (cols - 1)) == 0
    cb = _pick(n_conv, 256, 128)
    rb0 = row0 // t_len
    return pl.pallas_call(
        functools.partial(_conv_body, rows, cols),
        out_shape=jax.ShapeDtypeStruct((batch * t_len, n_conv), F32),
        grid=(batch, n_conv // cb),
        in_specs=[pl.BlockSpec((t_len, cb), lambda b, j: (rb0 + b, j)),
                  pl.BlockSpec((9, cb), lambda b, j: (0, j))],
        out_specs=pl.BlockSpec((t_len, cb), lambda b, j: (b, j)),
        compiler_params=_params("parallel", "arbitrary"),
        name="conv_silu",
    )(z, conv_w.reshape(9, n_conv))


def _time_block(d, tb, ntb):
    return tb + d * (ntb - 1 - 2 * tb)


def _dir_sign(d):
    return 1 - 2 * d


def _mlstm_body(has_init, want_final, ntb, *refs):
    it = iter(refs)
    q_ref, k_ref, v_ref, g_ref, gb_ref = (next(it) for _ in range(5))
    if has_init:
        c0_ref, n0_ref, m0_ref = (next(it) for _ in range(3))
    o_ref = next(it)
    if want_final:
        cf_ref, nf_ref, mf_ref = (next(it) for _ in range(3))
    c_s, n_s, m_s = (next(it) for _ in range(3))

    d = pl.program_id(2)
    tb = pl.program_id(3)
    L = MLSTM_CHUNK
    tbl, dk = q_ref.shape
    q_scale = dk ** -0.5

    @pl.when(tb == 0)
    def _():
        if has_init:
            c_s[...] = c0_ref[0, 0, 0]
            n_s[...] = n0_ref[0, 0, 0]
            m_s[...] = m0_ref[0, 0, 0]
        else:
            c_s[...] = jnp.zeros_like(c_s)
            n_s[...] = jnp.zeros_like(n_s)
            m_s[...] = jnp.zeros_like(m_s)

    sign = _dir_sign(d)
    sub = lax.broadcasted_iota(jnp.int32, (L, L), 0)
    lane = lax.broadcasted_iota(jnp.int32, (L, L), 1)
    sees = sign * (sub - lane) >= 0
    sees_t = sign * (lane - sub) >= 0
    eye = sub == lane
    gb = gb_ref[0, 0]
    ncl = tbl // L

    def chunk(ci, carry):
        lc = ci + d * (ncl - 1 - 2 * ci)
        r0 = pl.multiple_of(lc * L, L)
        rows = pl.ds(r0, L)
        qc = (q_ref[rows, :] * q_scale).astype(BF16)
        kc = k_ref[rows, :]
        vc = v_ref[rows, :].astype(BF16)
        gates = g_ref[0, 0, rows, :] + gb
        i_col = gates[:, 0:1]
        f_col = _log_sigmoid(gates[:, 1:2])
        m_st = m_s[...]
        b_row = jnp.sum(jnp.where(sees_t, f_col, 0.0), axis=0, keepdims=True)
        i_row = jnp.sum(jnp.where(eye, i_col, 0.0), axis=0, keepdims=True)
        b_col = jnp.sum(jnp.where(eye, b_row, 0.0), axis=1, keepdims=True)
        dmat = jnp.where(sees, b_col - b_row + i_row, -jnp.inf)
        m_inter = b_col + m_st
        m_t = jnp.maximum(m_inter, jnp.max(dmat, axis=1, keepdims=True))
        s = lax.dot_general(qc, kc.astype(BF16), (((1,), (1,)), ((), ())), preferred_element_type=F32)
        s = s * jnp.exp(dmat - m_t)
        e_inter = jnp.exp(m_inter - m_t)
        num = jnp.dot(s.astype(BF16), vc, preferred_element_type=F32)
        num = num + e_inter * jnp.dot(qc, c_s[...].astype(BF16), preferred_element_type=F32)
        qn = jnp.sum(qc.astype(F32) * n_s[...].astype(BF16).astype(F32), axis=1, keepdims=True)
        den = jnp.sum(s, axis=1, keepdims=True) + e_inter * qn
        o_ref[0, rows, :] = num / jnp.maximum(jnp.abs(den), jnp.exp(-m_t))
        b_end = jnp.sum(f_col, axis=0, keepdims=True)
        w = b_end - b_col + i_col
        m_new = jnp.maximum(b_end + m_st, jnp.max(w, axis=0, keepdims=True))
        decay = jnp.exp(b_end + m_st - m_new)
        kw = kc * jnp.exp(w - m_new)
        c_s[...] = decay * c_s[...] + lax.dot_general(kw.astype(BF16), vc, (((0,), (0,)), ((), ())),
                                                      preferred_element_type=F32)
        n_s[...] = decay * n_s[...] + jnp.sum(kw, axis=0, keepdims=True)
        m_s[...] = m_new
        return carry

    lax.fori_loop(0, ncl, chunk, 0)

    if want_final:
        @pl.when(tb == ntb - 1)
        def _():
            cf_ref[0, 0, 0] = c_s[...]
            nf_ref[0, 0, 0] = n_s[...]
            mf_ref[0, 0, 0] = m_s[...]


def _mlstm_scan(qk, z, gates, gate_b, init, row0, batch, t_len, heads, dk, dv, want_final):
    tbl = _pick(t_len, SCAN_TIME_BLOCK, MLSTM_CHUNK)
    ntb = t_len // tbl
    rb0 = row0 // tbl
    v_col0 = (2 * heads * dk) // dv

    def part_rows(b, h, d, tb):
        return b * ntb + _time_block(d, tb, ntb)

    in_specs = [
        pl.BlockSpec((tbl, dk), lambda b, h, d, tb: (part_rows(b, h, d, tb), h)),
        pl.BlockSpec((tbl, dk), lambda b, h, d, tb: (part_rows(b, h, d, tb), heads + h)),
        pl.BlockSpec((tbl, dv), lambda b, h, d, tb: (rb0 + part_rows(b, h, d, tb), v_col0 + h)),
        pl.BlockSpec((1, 1, tbl, 2), lambda b, h, d, tb: (d, h, rb0 + part_rows(b, h, d, tb), 0)),
        pl.BlockSpec((1, 1, 1, 2), lambda b, h, d, tb: (d, h, 0, 0)),
    ]
    args = [qk, qk, z, gates, gate_b]
    state_specs = [
        pl.BlockSpec((1, 1, 1, dk, dv), lambda b, h, d, tb: (b, d, h, 0, 0)),
        pl.BlockSpec((1, 1, 1, 1, dk), lambda b, h, d, tb: (b, d, h, 0, 0)),
        pl.BlockSpec((1, 1, 1, 1, 1), lambda b, h, d, tb: (b, d, h, 0, 0)),
    ]
    if init is not None:
        c0, n0, m0 = init
        in_specs += state_specs
        args += [c0, n0.reshape(batch, 2, heads, 1, dk), m0.reshape(batch, 2, heads, 1, 1)]
    out_shape = [jax.ShapeDtypeStruct((2, batch * t_len, heads * dv), F32)]
    out_specs = [pl.BlockSpec((1, tbl, dv), lambda b, h, d, tb: (d, part_rows(b, h, d, tb), h))]
    if want_final:
        out_shape += [jax.ShapeDtypeStruct((batch, 2, heads, dk, dv), F32),
                      jax.ShapeDtypeStruct((batch, 2, heads, 1, dk), F32),
                      jax.ShapeDtypeStruct((batch, 2, heads, 1, 1), F32)]
        out_specs += state_specs
    res = pl.pallas_call(
        functools.partial(_mlstm_body, init is not None, want_final, ntb),
        out_shape=out_shape,
        grid=(batch, heads, 2, ntb),
        in_specs=in_specs,
        out_specs=out_specs,
        scratch_shapes=[pltpu.VMEM((dk, dv), F32), pltpu.VMEM((1, dk), F32), pltpu.VMEM((1, 1), F32)],
        compiler_params=_params("parallel", "parallel", "arbitrary", "arbitrary"),
        name="mlstm_scan",
    )(*args)
    if want_final:
        out, cf, nf, mf = res
        return out, (cf, nf.reshape(batch, 2, heads, dk), mf.reshape(batch, 2, heads))
    return res[0], None


def _gated_body(kind, has_init, want_final, ntb, hpb, dk, dv, *refs):
    it = iter(refs)
    if kind == "hgrn":
        q_ref, v_ref, fpre_ref, fb_ref, lb_ref = (next(it) for _ in range(5))
    else:
        q_ref, k_ref, v_ref, alow_ref, wa_ref, ba_ref = (next(it) for _ in range(6))
    if has_init:
        s0_ref = next(it)
    o_ref = next(it)
    if want_final:
        sf_ref = next(it)
    st_s, lg_s = next(it), next(it)
    if kind == "hgrn":
        kk_s = next(it)

    d = pl.program_id(2)
    tb = pl.program_id(3)
    L = GATED_CHUNK
    tbl = q_ref.shape[0]
    q_scale = 1.0 if kind == "hgrn" else dk ** -0.5

    @pl.when(tb == 0)
    def _():
        for hh in range(hpb):
            if has_init:
                st_s[hh] = s0_ref[0, 0, hh].T
            else:
                st_s[hh] = jnp.zeros((dv, dk), F32)

    if kind == "hgrn":
        lb = lb_ref[0]
        f = lb + (1.0 - lb) * _sigmoid(fpre_ref[...] + fb_ref[0])
        lg_s[...] = jnp.log(f)
        kk_s[...] = 1.0 - f
    else:
        pre = jnp.dot(alow_ref[0].astype(BF16), wa_ref[0].astype(BF16), preferred_element_type=F32)
        lg_s[...] = _log_sigmoid(pre + ba_ref[0]) / GLA_TAU

    sign = _dir_sign(d)
    sub = lax.broadcasted_iota(jnp.int32, (L, L), 0)
    lane = lax.broadcasted_iota(jnp.int32, (L, L), 1)
    cum = jnp.where(sign * (sub - lane) >= 0, 1.0, 0.0).astype(F32)
    row_id = lax.broadcasted_iota(jnp.int32, (L, dk), 0)
    ncl = tbl // L

    def chunk(ci, carry):
        lc = ci + d * (ncl - 1 - 2 * ci)
        r0 = pl.multiple_of(lc * L, L)
        rows = pl.ds(r0, L)
        for hh in range(hpb):
            ks = slice(hh * dk, (hh + 1) * dk)
            vs = slice(hh * dv, (hh + 1) * dv)
            q = q_ref[rows, ks] * q_scale
            k = kk_s[rows, ks] if kind == "hgrn" else k_ref[rows, ks]
            v = v_ref[rows, vs].astype(BF16)
            g = lg_s[rows, ks]
            b = jnp.dot(cum, g, precision=lax.Precision.HIGHEST, preferred_element_type=F32)
            st = st_s[hh]
            o_inter = lax.dot_general((q * jnp.exp(b)).astype(BF16), st.astype(BF16),
                                      (((1,), (1,)), ((), ())), preferred_element_type=F32)
            a = jnp.zeros((L, L), F32)
            for s in range(L):
                seen = sign * (row_id - s) >= 0
                dec = jnp.exp(jnp.where(seen, b - b[s:s + 1, :], -jnp.inf))
                col = jnp.sum(q * (k[s:s + 1, :] * dec), axis=1, keepdims=True)
                a = jnp.where(lane == s, col, a)
            o_ref[0, rows, vs] = o_inter + jnp.dot(a.astype(BF16), v, preferred_element_type=F32)
            b_end = jnp.sum(g, axis=0, keepdims=True)
            ke = (k * jnp.exp(b_end - b)).astype(BF16)
            st_s[hh] = st * jnp.exp(b_end) + lax.dot_general(v, ke, (((0,), (0,)), ((), ())),
                                                            preferred_element_type=F32)
        return carry

    lax.fori_loop(0, ncl, chunk, 0)

    if want_final:
        @pl.when(tb == ntb - 1)
        def _():
            for hh in range(hpb):
                sf_ref[0, 0, hh] = st_s[hh].T


def _gated_scan(kind, inputs, init, row0, batch, t_len, heads, dk, dv, hpb, want_final):
    tbl = _pick(t_len, SCAN_TIME_BLOCK, GATED_CHUNK)
    ntb = t_len // tbl
    rb0 = row0 // tbl
    ng = heads // hpb
    kw, vw = hpb * dk, hpb * dv

    def part_rows(b, g, d, tb):
        return b * ntb + _time_block(d, tb, ntb)

    def full_rows(b, g, d, tb):
        return rb0 + part_rows(b, g, d, tb)

    if kind == "hgrn":
        qi, z, f_b, lb = inputs
        f_col0 = (heads * dk + heads * dv) // kw
        in_specs = [
            pl.BlockSpec((tbl, kw), lambda b, g, d, tb: (part_rows(b, g, d, tb), g)),
            pl.BlockSpec((tbl, vw), lambda b, g, d, tb: (part_rows(b, g, d, tb), (heads * dk) // vw + g)),
            pl.BlockSpec((tbl, kw), lambda b, g, d, tb: (full_rows(b, g, d, tb), f_col0 + d * ng + g)),
            pl.BlockSpec((1, 1, kw), lambda b, g, d, tb: (d, 0, g)),
            pl.BlockSpec((1, 1, kw), lambda b, g, d, tb: (d, 0, g)),
        ]
        args = [qi, qi, z, f_b, lb]
    else:
        qk, z, a_low, w_a2, b_a = inputs
        rank = a_low.shape[-1]
        in_specs = [
            pl.BlockSpec((tbl, kw), lambda b, g, d, tb: (part_rows(b, g, d, tb), g)),
            pl.BlockSpec((tbl, kw), lambda b, g, d, tb: (part_rows(b, g, d, tb), ng + g)),
            pl.BlockSpec((tbl, vw), lambda b, g, d, tb: (full_rows(b, g, d, tb), (2 * heads * dk) // vw + g)),
            pl.BlockSpec((1, tbl, rank), lambda b, g, d, tb: (d, full_rows(b, g, d, tb), 0)),
            pl.BlockSpec((1, rank, kw), lambda b, g, d, tb: (d, 0, g)),
            pl.BlockSpec((1, 1, kw), lambda b, g, d, tb: (d, 0, g)),
        ]
        args = [qk, qk, z, a_low, w_a2, b_a]
    state_spec = pl.BlockSpec((1, 1, hpb, dk, dv), lambda b, g, d, tb: (b, d, g, 0, 0))
    if init is not None:
        in_specs.append(state_spec)
        args.append(init)
    out_shape = [jax.ShapeDtypeStruct((2, batch * t_len, heads * dv), F32)]
    out_specs = [pl.BlockSpec((1, tbl, vw), lambda b, g, d, tb: (d, part_rows(b, g, d, tb), g))]
    if want_final:
        out_shape.append(jax.ShapeDtypeStruct((batch, 2, heads, dk, dv), F32))
        out_specs.append(state_spec)
    scratch = [pltpu.VMEM((hpb, dv, dk), F32), pltpu.VMEM((tbl, kw), F32)]
    if kind == "hgrn":
        scratch.append(pltpu.VMEM((tbl, kw), F32))
    res = pl.pallas_call(
        functools.partial(_gated_body, kind, init is not None, want_final, ntb, hpb, dk, dv),
        out_shape=out_shape,
        grid=(batch, ng, 2, ntb),
        in_specs=in_specs,
        out_specs=out_specs,
        scratch_shapes=scratch,
        compiler_params=_params("parallel", "parallel", "arbitrary", "arbitrary"),
        name=kind + "_scan",
    )(*args)
    return (res[0], res[1]) if want_final else (res[0], None)


def _head_norm_body(gate_kind, heads, dv, o_ref, z_ref, g_ref, y_ref):
    for h in range(heads):
        sl = slice(h * dv, (h + 1) * dv)
        o = o_ref[0, :, sl] + o_ref[1, :, sl]
        o = o * lax.rsqrt(jnp.mean(o * o, axis=-1, keepdims=True) + EPS) * g_ref[:, sl]
        zg = z_ref[:, sl]
        gate = _sigmoid(zg) if gate_kind == "sigmoid" else _silu(zg)
        y_ref[:, sl] = (gate * o).astype(y_ref.dtype)


def _head_norm_gate(out2, z, gate_col0, norm_g, row0, gate_kind):
    _, n, hv = out2.shape
    heads, dv = norm_g.shape
    bm = _pick(n, ROW_BLOCK, 8)
    rb0 = row0 // bm
    return pl.pallas_call(
        functools.partial(_head_norm_body, gate_kind, heads, dv),
        out_shape=jax.ShapeDtypeStruct((n, hv), BF16),
        grid=(n // bm,),
        in_specs=[pl.BlockSpec((2, bm, hv), lambda i: (0, i, 0)),
                  pl.BlockSpec((bm, hv), lambda i: (rb0 + i, gate_col0 // hv)),
                  pl.BlockSpec((1, hv), lambda i: (0, 0))],
        out_specs=pl.BlockSpec((bm, hv), lambda i: (i, 0)),
        compiler_params=_params("parallel"),
        name="head_norm_gate",
    )(out2, z, norm_g.reshape(1, hv))


def kernel(x_prompt, x_sample, c, c_ctx, state_mlstm_C, state_mlstm_n, state_mlstm_m, state_hgrn_S,
           state_gla_S, mod_w, mod_b, norm_g, ffn_w_gate, ffn_w_up, ffn_w_down, mlstm_w_in, mlstm_conv,
           mlstm_gate_b, mlstm_norm_g, mlstm_w_out, hgrn_w_in, hgrn_conv, hgrn_f_b, hgrn_lb_logits,
           hgrn_norm_g, hgrn_w_out, gla_w_in, gla_conv, gla_w_a2, gla_b_a, gla_norm_g, gla_w_out,
           final_norm_g):
    bp, tp, d = x_prompt.shape
    bs, ts, _ = x_sample.shape
    n_p, n_s = bp * tp, bs * ts
    depth = mod_w.shape[0]
    a_heads, a_dk, a_dv = state_mlstm_C.shape[3:]
    b_heads, b_dk, b_dv = state_hgrn_S.shape[3:]
    c_heads, c_dk, c_dv = state_gla_S.shape[3:]
    assert 1 + bs <= COND_ROWS and n_p % ts == 0

    x = jnp.concatenate([x_prompt.reshape(n_p, d), x_sample.reshape(n_s, d)], axis=0)
    cond = jnp.zeros((COND_ROWS, d), F32).at[0].set(c_ctx).at[1:1 + bs].set(c)
    mod = _modulation(cond, mod_w, mod_b).reshape(depth, COND_ROWS, N_MOD, d)
    cond_of = _CondRows(n_p, ts)
    parts = ((0, bp, tp, 1, True), (n_p, bs, ts, ts // GRID_W, False))

    lb_p = jax.nn.softmax(hgrn_lb_logits.astype(F32), axis=1)
    lb_all = jnp.cumsum(lb_p, axis=1) - lb_p[:, :1]

    def swiglu(x, l, s, sub):
        h = _norm_mod(x, norm_g[l, sub], mod[l], sub, cond_of)
        a = _swiglu_up(h, ffn_w_gate[l, s].astype(BF16), ffn_w_up[l, s].astype(BF16))
        return _matmul_residual(a, ffn_w_down[l, s].astype(BF16), x, mod[l], 3 * sub + 2, 0.5, cond_of)

    def mlstm(h, j):
        nq, nv = a_heads * a_dk, a_heads * a_dv
        w_in = mlstm_w_in[j]
        z = _matmul(h, w_in[:, :2 * nq + 2 * nv].astype(BF16), F32)
        n_tail = w_in.shape[1] - (2 * nq + 2 * nv)
        w_tail = jnp.pad(w_in[:, 2 * nq + 2 * nv:], ((0, 0), (0, 128 - n_tail))).astype(BF16)
        zt = _matmul(h, w_tail, F32, bn_pref=128)[:, :n_tail]
        gates = zt.reshape(-1, 2, 2, a_heads).transpose(1, 3, 0, 2)
        gate_b = mlstm_gate_b[j].astype(F32).transpose(0, 2, 1).reshape(2, a_heads, 1, 2)
        ys, finals = [], None
        for row0, batch, t_len, rows, is_ctx in parts:
            qk = _conv_silu(z, mlstm_conv[j], 2 * nq, row0, batch, t_len, rows)
            init = None if is_ctx else (state_mlstm_C[:, j], state_mlstm_n[:, j], state_mlstm_m[:, j])
            out2, fin = _mlstm_scan(qk, z, gates, gate_b, init, row0, batch, t_len, a_heads, a_dk, a_dv, is_ctx)
            ys.append(_head_norm_gate(out2, z, 2 * nq + nv, mlstm_norm_g[j], row0, "sigmoid"))
            finals = fin if is_ctx else finals
        return jnp.concatenate(ys, axis=0), mlstm_w_out[j], finals

    def hgrn(h, j, l):
        nq, ni = b_heads * b_dk, b_heads * b_dv
        z = _matmul(h, hgrn_w_in[j].astype(BF16), F32)
        f_b = hgrn_f_b[j].astype(F32).reshape(2, 1, nq)
        lb = lb_all[:, l].reshape(2, 1, nq)
        hpb = 2 if b_heads % 2 == 0 else 1
        ys, finals = [], None
        for row0, batch, t_len, rows, is_ctx in parts:
            qi = _conv_silu(z, hgrn_conv[j], nq + ni, row0, batch, t_len, rows)
            init = None if is_ctx else state_hgrn_S[:, j]
            out2, fin = _gated_scan("hgrn", (qi, z, f_b, lb), init, row0, batch, t_len,
                                    b_heads, b_dk, b_dv, hpb, is_ctx)
            ys.append(_head_norm_gate(out2, z, 3 * nq + ni, hgrn_norm_g[j], row0, "silu"))
            finals = (fin,) if is_ctx else finals
        return jnp.concatenate(ys, axis=0), hgrn_w_out[j], finals

    def gla(h, j):
        nq, nv = c_heads * c_dk, c_heads * c_dv
        w_in = gla_w_in[j]
        z = _matmul(h, w_in[:, :2 * nq + 2 * nv].astype(BF16), F32)
        n_tail = w_in.shape[1] - (2 * nq + 2 * nv)
        w_tail = jnp.pad(w_in[:, 2 * nq + 2 * nv:], ((0, 0), (0, 128 - n_tail))).astype(BF16)
        zt = _matmul(h, w_tail, F32, bn_pref=128)[:, :n_tail]
        a_low = zt.reshape(-1, 2, n_tail // 2).transpose(1, 0, 2)
        b_a = gla_b_a[j].astype(F32).reshape(2, 1, nq)
        ys, finals = [], None
        for row0, batch, t_len, rows, is_ctx in parts:
            qk = _conv_silu(z, gla_conv[j], 2 * nq, row0, batch, t_len, rows)
            init = None if is_ctx else state_gla_S[:, j]
            out2, fin = _gated_scan("gla", (qk, z, a_low, gla_w_a2[j], b_a), init, row0, batch, t_len,
                                    c_heads, c_dk, c_dv, 1, is_ctx)
            ys.append(_head_norm_gate(out2, z, 2 * nq + nv, gla_norm_g[j], row0, "silu"))
            finals = (fin,) if is_ctx else finals
        return jnp.concatenate(ys, axis=0), gla_w_out[j], finals

    states = []
    for l in range(depth):
        x = swiglu(x, l, 0, 0)
        h = _norm_mod(x, norm_g[l, 1], mod[l], 1, cond_of)
        j, kind = l // 3, l % 3
        y, w_out, fin = mlstm(h, j) if kind == 0 else hgrn(h, j, l) if kind == 1 else gla(h, j)
        x = _matmul_residual(y, w_out.astype(BF16), x, mod[l], 5, 1.0, cond_of)
        x = swiglu(x, l, 1, 2)
        states.append(fin)

    y = _final_norm(x, final_norm_g)
    y_prompt = y[:n_p].reshape(bp, tp, d)
    y_sample = y[n_p:].reshape(bs, ts, d)
    new_c = jnp.stack([states[l][0] for l in range(0, depth, 3)], axis=1)
    new_n = jnp.stack([states[l][1] for l in range(0, depth, 3)], axis=1)
    new_m = jnp.stack([states[l][2] for l in range(0, depth, 3)], axis=1)
    new_hgrn = jnp.stack([states[l][0] for l in range(1, depth, 3)], axis=1)
    new_gla = jnp.stack([states[l][0] for l in range(2, depth, 3)], axis=1)
    return (y_prompt, y_sample, new_c, new_n, new_m, new_hgrn, new_gla)
```

```python
import functools

import jax
import jax.numpy as jnp
import numpy as np
from jax import lax
from jax.experimental import pallas as pl
from jax.experimental.pallas import tpu as pltpu

F32 = jnp.float32
BF16 = jnp.bfloat16

GRID_W = 64
EPS = 1e-6
GLA_TAU = 16.0
MLSTM_CHUNK = 64
GATED_CHUNK = {"hgrn": 128, "gla": 128}
N_MOD = 9
COND_ROWS = 8

VMEM_LIMIT = 56 * 1024 * 1024
SCAN_TIME_BLOCK = 512
ROW_BLOCK = 256


def _params(*sem):
    return pltpu.CompilerParams(dimension_semantics=sem, vmem_limit_bytes=VMEM_LIMIT)


def _pick(n, pref, align):
    if n <= pref:
        return n
    best = None
    for cand in range(align, pref + 1, align):
        if n % cand == 0:
            best = cand
    assert best is not None, (n, pref, align)
    return best


def _sigmoid(x):
    return 1.0 / (1.0 + jnp.exp(-x))


def _silu(x):
    return x * _sigmoid(x)


def _log_sigmoid(x):
    return jnp.minimum(x, 0.0) - jnp.log1p(jnp.exp(-jnp.abs(x)))


class _CondRows:
    def __init__(self, n_prompt_rows, t_sample):
        assert n_prompt_rows % t_sample == 0
        self.n_prompt_rows, self.t_sample = n_prompt_rows, t_sample

    def block_rows(self, pref):
        return _pick(self.t_sample, pref, 8)

    def index(self, bm):
        assert self.t_sample % bm == 0

        def f(i):
            start = i * bm
            return jnp.where(start < self.n_prompt_rows, 0, 1 + (start - self.n_prompt_rows) // self.t_sample)
        return f


def _mod_body(c_ref, w_ref, b_ref, o_ref):
    a = _silu(c_ref[...]).astype(BF16)
    o_ref[0] = jnp.dot(a, w_ref[0].astype(BF16), preferred_element_type=F32) + b_ref[0]


def _modulation(cond, mod_w, mod_b):
    depth, d, n = mod_w.shape
    bn = _pick(n, 512, 128)
    return pl.pallas_call(
        _mod_body,
        out_shape=jax.ShapeDtypeStruct((depth, COND_ROWS, n), F32),
        grid=(depth, n // bn),
        in_specs=[pl.BlockSpec((COND_ROWS, d), lambda l, j: (0, 0)),
                  pl.BlockSpec((1, d, bn), lambda l, j: (l, 0, j)),
                  pl.BlockSpec((1, 1, bn), lambda l, j: (l, 0, j))],
        out_specs=pl.BlockSpec((1, COND_ROWS, bn), lambda l, j: (l, 0, j)),
        compiler_params=_params("arbitrary", "arbitrary"),
        name="modulation",
    )(cond, mod_w, mod_b.reshape(depth, 1, n))


def _norm_mod_body(sub, x_ref, g_ref, m_ref, o_ref):
    x = x_ref[...]
    y = x * lax.rsqrt(jnp.mean(x * x, axis=-1, keepdims=True) + EPS) * g_ref[...]
    m = m_ref[0]
    shift = m[3 * sub:3 * sub + 1, :]
    scale = m[3 * sub + 1:3 * sub + 2, :]
    o_ref[...] = (y * (1.0 + scale) + shift).astype(o_ref.dtype)


def _norm_mod(x, g, mod_l, sub, cond_of):
    n, d = x.shape
    bm = cond_of.block_rows(ROW_BLOCK)
    cr = cond_of.index(bm)
    return pl.pallas_call(
        functools.partial(_norm_mod_body, sub),
        out_shape=jax.ShapeDtypeStruct((n, d), BF16),
        grid=(n // bm,),
        in_specs=[pl.BlockSpec((bm, d), lambda i: (i, 0)),
                  pl.BlockSpec((1, d), lambda i: (0, 0)),
                  pl.BlockSpec((1, N_MOD, d), lambda i: (cr(i), 0, 0))],
        out_specs=pl.BlockSpec((bm, d), lambda i: (i, 0)),
        compiler_params=_params("parallel"),
        name="norm_mod",
    )(x, g.reshape(1, d), mod_l)


def _final_norm_body(x_ref, g_ref, o_ref):
    x = x_ref[...]
    o_ref[...] = x * lax.rsqrt(jnp.mean(x * x, axis=-1, keepdims=True) + EPS) * g_ref[...]


def _final_norm(x, g):
    n, d = x.shape
    bm = _pick(n, ROW_BLOCK, 8)
    return pl.pallas_call(
        _final_norm_body,
        out_shape=jax.ShapeDtypeStruct((n, d), F32),
        grid=(n // bm,),
        in_specs=[pl.BlockSpec((bm, d), lambda i: (i, 0)), pl.BlockSpec((1, d), lambda i: (0, 0))],
        out_specs=pl.BlockSpec((bm, d), lambda i: (i, 0)),
        compiler_params=_params("parallel"),
        name="final_norm",
    )(x, g.reshape(1, d))


def _matmul_body(x_ref, w_ref, o_ref):
    o_ref[...] = jnp.dot(x_ref[...], w_ref[...], preferred_element_type=F32).astype(o_ref.dtype)


def _matmul(x, w, out_dtype, bm_pref=1024, bn_pref=512):
    n, k = x.shape
    p = w.shape[1]
    bm = _pick(n, bm_pref, 8)
    bn = _pick(p, bn_pref, 128)
    return pl.pallas_call(
        _matmul_body,
        out_shape=jax.ShapeDtypeStruct((n, p), out_dtype),
        grid=(n // bm, p // bn),
        in_specs=[pl.BlockSpec((bm, k), lambda i, j: (i, 0)),
                  pl.BlockSpec((k, bn), lambda i, j: (0, j))],
        out_specs=pl.BlockSpec((bm, bn), lambda i, j: (i, j)),
        compiler_params=_params("parallel", "arbitrary"),
        name="matmul",
    )(x, w)


def _swiglu_up_body(h_ref, wg_ref, wu_ref, o_ref):
    h = h_ref[...]
    g = jnp.dot(h, wg_ref[...], preferred_element_type=F32)
    u = jnp.dot(h, wu_ref[...], preferred_element_type=F32)
    o_ref[...] = (_silu(g) * u).astype(o_ref.dtype)


def _swiglu_up(h, wg, wu, bm_pref=1024, bn_pref=256):
    n, k = h.shape
    f = wg.shape[1]
    bm = _pick(n, bm_pref, 8)
    bn = _pick(f, bn_pref, 128)
    return pl.pallas_call(
        _swiglu_up_body,
        out_shape=jax.ShapeDtypeStruct((n, f), BF16),
        grid=(n // bm, f // bn),
        in_specs=[pl.BlockSpec((bm, k), lambda i, j: (i, 0)),
                  pl.BlockSpec((k, bn), lambda i, j: (0, j)),
                  pl.BlockSpec((k, bn), lambda i, j: (0, j))],
        out_specs=pl.BlockSpec((bm, bn), lambda i, j: (i, j)),
        compiler_params=_params("parallel", "arbitrary"),
        name="swiglu_up",
    )(h, wg, wu)


def _matmul_residual_body(gate_row, coef, a_ref, w_ref, x_ref, m_ref, o_ref):
    y = jnp.dot(a_ref[...], w_ref[...], preferred_element_type=F32)
    gate = m_ref[0][gate_row:gate_row + 1, :]
    o_ref[...] = x_ref[...] + (coef * gate) * y


def _matmul_residual(a, w, x, mod_l, gate_row, coef, cond_of, bm_pref=512, bn_pref=512):
    n, k = a.shape
    d = w.shape[1]
    bm = cond_of.block_rows(bm_pref)
    bn = _pick(d, bn_pref, 128)
    cr = cond_of.index(bm)
    return pl.pallas_call(
        functools.partial(_matmul_residual_body, gate_row, coef),
        out_shape=jax.ShapeDtypeStruct((n, d), F32),
        grid=(n // bm, d // bn),
        in_specs=[pl.BlockSpec((bm, k), lambda i, j: (i, 0)),
                  pl.BlockSpec((k, bn), lambda i, j: (0, j)),
                  pl.BlockSpec((bm, bn), lambda i, j: (i, j)),
                  pl.BlockSpec((1, N_MOD, bn), lambda i, j: (cr(i), 0, j))],
        out_specs=pl.BlockSpec((bm, bn), lambda i, j: (i, j)),
        compiler_params=_params("parallel", "arbitrary"),
        name="matmul_residual",
    )(a, w, x, mod_l)


def _conv_body(rows, cols, z_ref, w_ref, o_ref):
    u = z_ref[...]
    t_len = u.shape[0]
    t = lax.broadcasted_iota(jnp.int32, (t_len, 1), 0)
    if rows == 1:
        r, c = jnp.zeros_like(t), t
    else:
        r, c = t >> (cols.bit_length() - 1), t & (cols - 1)
    w = w_ref[...]
    acc = jnp.zeros_like(u)
    for i in range(3):
        if rows == 1 and i != 1:
            continue
        for j in range(3):
            off = (i - 1) * cols + (j - 1)
            shifted = u if off == 0 else pltpu.roll(u, (-off) % t_len, 0)
            ok = (r + (i - 1) >= 0) & (r + (i - 1) < rows) & (c + (j - 1) >= 0) & (c + (j - 1) < cols)
            acc = acc + jnp.where(ok, shifted, 0.0) * w[3 * i + j:3 * i + j + 1, :]
    o_ref[...] = _silu(acc)


def _conv_silu(z, conv_w, n_conv, row0, batch, t_len, rows):
    cols = t_len // rows
    assert rows == 1 or (cols & (cols - 1)) == 0
    cb = _pick(n_conv, 256, 128)
    rb0 = row0 // t_len
    return pl.pallas_call(
        functools.partial(_conv_body, rows, cols),
        out_shape=jax.ShapeDtypeStruct((batch * t_len, n_conv), F32),
        grid=(batch, n_conv // cb),
        in_specs=[pl.BlockSpec((t_len, cb), lambda b, j: (rb0 + b, j)),
                  pl.BlockSpec((9, cb), lambda b, j: (0, j))],
        out_specs=pl.BlockSpec((t_len, cb), lambda b, j: (b, j)),
        compiler_params=_params("parallel", "arbitrary"),
        name="conv_silu",
    )(z, conv_w.reshape(9, n_conv))


def _time_block(d, tb, ntb):
    return tb + d * (ntb - 1 - 2 * tb)


def _dir_sign(d):
    return 1 - 2 * d


def _mlstm_body(has_init, want_final, ntb, hpb, dk, dv, *refs):
    it = iter(refs)
    q_ref, k_ref, v_ref, g_ref, gb_ref = (next(it) for _ in range(5))
    if has_init:
        c0_ref, n0_ref, m0_ref = (next(it) for _ in range(3))
    o_ref = next(it)
    if want_final:
        cf_ref, nf_ref, mf_ref = (next(it) for _ in range(3))
    c_s, n_s, m_s = (next(it) for _ in range(3))

    d = pl.program_id(2)
    tb = pl.program_id(3)
    L = MLSTM_CHUNK
    tbl = q_ref.shape[0]
    q_scale = dk ** -0.5

    @pl.when(tb == 0)
    def _():
        if has_init:
            c_s[...] = c0_ref[0, 0]
            n_s[...] = n0_ref[0, 0]
            m_s[...] = m0_ref[0, 0]
        else:
            c_s[...] = jnp.zeros_like(c_s)
            n_s[...] = jnp.zeros_like(n_s)
            m_s[...] = jnp.zeros_like(m_s)

    sign = _dir_sign(d)
    sub = lax.broadcasted_iota(jnp.int32, (L, L), 0)
    lane = lax.broadcasted_iota(jnp.int32, (L, L), 1)
    sees = sign * (sub - lane) >= 0
    sees_t = sign * (lane - sub) >= 0
    eye = sub == lane
    ncl = tbl // L

    def chunk(ci, carry):
        lc = ci + d * (ncl - 1 - 2 * ci)
        r0 = pl.multiple_of(lc * L, L)
        rows = pl.ds(r0, L)
        loaded = []
        for hh in range(hpb):
            ks = slice(hh * dk, (hh + 1) * dk)
            vs = slice(hh * dv, (hh + 1) * dv)
            loaded.append((q_ref[rows, ks], k_ref[rows, ks], v_ref[rows, vs], g_ref[0, hh, rows, :] + gb_ref[0, hh],
                           c_s[hh], n_s[hh], m_s[hh]))
        results = []
        for q_raw, kc, v_raw, gates, c_st, n_st, m_st in loaded:
            qc = (q_raw * q_scale).astype(BF16)
            vc = v_raw.astype(BF16)
            i_col = gates[:, 0:1]
            f_col = _log_sigmoid(gates[:, 1:2])
            b_row = jnp.sum(jnp.where(sees_t, f_col, 0.0), axis=0, keepdims=True)
            i_row = jnp.sum(jnp.where(eye, i_col, 0.0), axis=0, keepdims=True)
            b_col = jnp.sum(jnp.where(eye, b_row, 0.0), axis=1, keepdims=True)
            dmat = jnp.where(sees, b_col - b_row + i_row, -jnp.inf)
            m_inter = b_col + m_st
            m_t = jnp.maximum(m_inter, jnp.max(dmat, axis=1, keepdims=True))
            s = _dot_nt(qc, kc.astype(BF16)) * jnp.exp(dmat - m_t)
            e_inter = jnp.exp(m_inter - m_t)
            num = jnp.dot(s.astype(BF16), vc, preferred_element_type=F32)
            num = num + e_inter * jnp.dot(qc, c_st.astype(BF16), preferred_element_type=F32)
            qn = jnp.sum(qc.astype(F32) * n_st.astype(BF16).astype(F32), axis=1, keepdims=True)
            den = jnp.sum(s, axis=1, keepdims=True) + e_inter * qn
            out = num / jnp.maximum(jnp.abs(den), jnp.exp(-m_t))
            b_end = jnp.sum(f_col, axis=0, keepdims=True)
            w = b_end - b_col + i_col
            m_new = jnp.maximum(b_end + m_st, jnp.max(w, axis=0, keepdims=True))
            decay = jnp.exp(b_end + m_st - m_new)
            kw = kc * jnp.exp(w - m_new)
            results.append((out, decay * c_st + _dot_tn(kw.astype(BF16), vc),
                            decay * n_st + jnp.sum(kw, axis=0, keepdims=True), m_new))
        for hh, (out, c_new, n_new, m_new) in enumerate(results):
            o_ref[0, rows, hh * dv:(hh + 1) * dv] = out
            c_s[hh] = c_new
            n_s[hh] = n_new
            m_s[hh] = m_new
        return carry

    lax.fori_loop(0, ncl, chunk, 0)

    if want_final:
        @pl.when(tb == ntb - 1)
        def _():
            cf_ref[0, 0] = c_s[...]
            nf_ref[0, 0] = n_s[...]
            mf_ref[0, 0] = m_s[...]


def _mlstm_scan(qk, z, gates, gate_b, init, row0, batch, t_len, heads, dk, dv, hpb, want_final):
    tbl = _pick(t_len, SCAN_TIME_BLOCK, MLSTM_CHUNK)
    ntb = t_len // tbl
    rb0 = row0 // tbl
    ng = heads // hpb
    v_col0 = (2 * heads * dk) // (hpb * dv)

    def part_rows(b, h, d, tb):
        return b * ntb + _time_block(d, tb, ntb)

    in_specs = [
        pl.BlockSpec((tbl, hpb * dk), lambda b, h, d, tb: (part_rows(b, h, d, tb), h)),
        pl.BlockSpec((tbl, hpb * dk), lambda b, h, d, tb: (part_rows(b, h, d, tb), ng + h)),
        pl.BlockSpec((tbl, hpb * dv), lambda b, h, d, tb: (rb0 + part_rows(b, h, d, tb), v_col0 + h)),
        pl.BlockSpec((1, hpb, tbl, 2), lambda b, h, d, tb: (d, h, rb0 + part_rows(b, h, d, tb), 0)),
        pl.BlockSpec((1, hpb, 1, 2), lambda b, h, d, tb: (d, h, 0, 0)),
    ]
    args = [qk, qk, z, gates, gate_b]
    state_specs = [
        pl.BlockSpec((1, 1, hpb, dk, dv), lambda b, h, d, tb: (b, d, h, 0, 0)),
        pl.BlockSpec((1, 1, hpb, 1, dk), lambda b, h, d, tb: (b, d, h, 0, 0)),
        pl.BlockSpec((1, 1, hpb, 1, 1), lambda b, h, d, tb: (b, d, h, 0, 0)),
    ]
    if init is not None:
        c0, n0, m0 = init
        in_specs += state_specs
        args += [c0, n0.reshape(batch, 2, heads, 1, dk), m0.reshape(batch, 2, heads, 1, 1)]
    out_shape = [jax.ShapeDtypeStruct((2, batch * t_len, heads * dv), F32)]
    out_specs = [pl.BlockSpec((1, tbl, hpb * dv), lambda b, h, d, tb: (d, part_rows(b, h, d, tb), h))]
    if want_final:
        out_shape += [jax.ShapeDtypeStruct((batch, 2, heads, dk, dv), F32),
                      jax.ShapeDtypeStruct((batch, 2, heads, 1, dk), F32),
                      jax.ShapeDtypeStruct((batch, 2, heads, 1, 1), F32)]
        out_specs += state_specs
    res = pl.pallas_call(
        functools.partial(_mlstm_body, init is not None, want_final, ntb, hpb, dk, dv),
        out_shape=out_shape,
        grid=(batch, ng, 2, ntb),
        in_specs=in_specs,
        out_specs=out_specs,
        scratch_shapes=[pltpu.VMEM((hpb, dk, dv), F32), pltpu.VMEM((hpb, 1, dk), F32),
                        pltpu.VMEM((hpb, 1, 1), F32)],
        compiler_params=_params("parallel", "parallel", "arbitrary", "arbitrary"),
        name="mlstm_scan",
    )(*args)
    if want_final:
        out, cf, nf, mf = res
        return out, (cf, nf.reshape(batch, 2, heads, dk), mf.reshape(batch, 2, heads))
    return res[0], None


def _pair_masks(L, nh):
    p = L.bit_length() - 1
    t = np.arange(L)[:, None]
    s = np.arange(L)[None, :]
    out = np.zeros((2, p + 1, L, L), np.float32)
    for j in range(p):
        same = (t >> (j + 1)) == (s >> (j + 1))
        t_hi, s_hi = ((t >> j) & 1) == 1, ((s >> j) & 1) == 1
        out[0, j] = same & t_hi & ~s_hi
        out[1, j] = same & ~t_hi & s_hi
    out[:, p] = np.eye(L, dtype=np.float32)
    return jnp.asarray(np.tile(out, (1, 1, 1, nh)).reshape(2, (p + 1) * L, nh * L))


def _block_rows(x, blk, r):
    n, w = x.shape
    if blk >= 16:
        return jnp.concatenate([jnp.broadcast_to(x[b0 + r:b0 + r + 1, :], (blk, w)) for b0 in range(0, n, blk)],
                               axis=0)
    x3 = x.reshape(n // 8, 8, w)
    sub = lax.broadcasted_iota(jnp.int32, (1, 8, 1), 1)
    out = None
    for s0 in range(0, 8, blk):
        bc = jnp.broadcast_to(x3[:, s0 + r:s0 + r + 1, :], x3.shape)
        out = bc if out is None else jnp.where(sub >= s0, bc, out)
    return out.reshape(n, w)


def _head_diag_rows(x, nh, w):
    if nh == 1:
        return x
    lane_head = lax.broadcasted_iota(jnp.int32, (1, nh * w), 1) // w
    return jnp.concatenate([jnp.where(lane_head == h, x, jnp.zeros_like(x)) for h in range(nh)], axis=0)


def _dot_nt(a, b):
    return lax.dot_general(a, b, (((1,), (1,)), ((), ())), preferred_element_type=F32)


def _dot_tn(a, b):
    return lax.dot_general(a, b, (((0,), (0,)), ((), ())), preferred_element_type=F32)


def _gated_body(kind, has_init, want_final, ntb, ngrp, nh, dk, dv, L, *refs):
    it = iter(refs)
    if kind == "hgrn":
        q_ref, v_ref, fpre_ref, fb_ref, lb_ref = (next(it) for _ in range(5))
    else:
        q_ref, k_ref, v_ref, alow_ref, wa_ref, ba_ref = (next(it) for _ in range(6))
    m_ref = next(it)
    if has_init:
        s0_ref = next(it)
    o_ref = next(it)
    if want_final:
        sf_ref = next(it)
    st_s, lg_s = next(it), next(it)
    if kind == "hgrn":
        kk_s = next(it)

    d = pl.program_id(2)
    tb = pl.program_id(3)
    tbl = q_ref.shape[0]
    p = L.bit_length() - 1
    kw, vw = nh * dk, nh * dv
    q_scale = 1.0 if kind == "hgrn" else dk ** -0.5

    @pl.when(tb == 0)
    def _():
        st_s[...] = jnp.zeros_like(st_s)
        if has_init:
            for gi in range(ngrp):
                for h in range(nh):
                    st_s[gi, h * dv:(h + 1) * dv, h * dk:(h + 1) * dk] = s0_ref[0, 0, gi * nh + h].T

    if kind == "hgrn":
        lb = lb_ref[0]
        f = lb + (1.0 - lb) * _sigmoid(fpre_ref[...] + fb_ref[0])
        lg_s[...] = jnp.log(f)
        kk_s[...] = 1.0 - f
    else:
        pre = jnp.dot(alow_ref[0].astype(BF16), wa_ref[0].astype(BF16), preferred_element_type=F32)
        lg_s[...] = _log_sigmoid(pre + ba_ref[0]) / GLA_TAU

    row = lax.broadcasted_iota(jnp.int32, (L, 1), 0)
    if nh > 1:
        own = ((lax.broadcasted_iota(jnp.int32, (vw, kw), 0) // dv)
               == (lax.broadcasted_iota(jnp.int32, (vw, kw), 1) // dk))
    ncl = tbl // L

    def run(rev):
        def chunk(ci, carry):
            r0 = pl.multiple_of(((ncl - 1 - ci) if rev else ci) * L, L)
            rows = pl.ds(r0, L)
            loaded = []
            for gi in range(ngrp):
                ks = slice(gi * kw, (gi + 1) * kw)
                vs = slice(gi * vw, (gi + 1) * vw)
                loaded.append((q_ref[rows, ks], kk_s[rows, ks] if kind == "hgrn" else k_ref[rows, ks],
                               v_ref[rows, vs], lg_s[rows, ks], st_s[gi]))
            results = []
            for q, k, v, seg, st in loaded:
                q = q * q_scale
                v = v.astype(BF16)
                a = m_ref[0, p * L:(p + 1) * L, :] * _dot_nt(q.astype(BF16), _head_diag_rows(k.astype(BF16), nh, dk))
                for j in range(p):
                    half = 1 << j
                    edge = _block_rows(seg, 2 * half, half if rev else half - 1)
                    later = (((row >> j) & 1) == 1) != rev
                    z = (jnp.exp(jnp.where(later, seg, edge - seg)) * jnp.where(later, q, k)).astype(BF16)
                    a = a + m_ref[0, j * L:(j + 1) * L, :] * _dot_nt(z, _head_diag_rows(z, nh, dk))
                    seg = seg + jnp.where(later, edge, 0.0)
                total = seg[0:1, :] if rev else seg[L - 1:L, :]
                out = (jnp.dot(a.astype(BF16), _head_diag_rows(v, nh, dv), preferred_element_type=F32)
                       + _dot_nt((q * jnp.exp(seg)).astype(BF16), st.astype(BF16)))
                st = st * jnp.exp(total) + _dot_tn(v, (k * jnp.exp(total - seg)).astype(BF16))
                results.append((out, st if nh == 1 else jnp.where(own, st, 0.0)))
            for gi, (out, st) in enumerate(results):
                o_ref[0, rows, gi * vw:(gi + 1) * vw] = out
                st_s[gi] = st
            return carry

        lax.fori_loop(0, ncl, chunk, 0)

    @pl.when(d == 0)
    def _():
        run(False)

    @pl.when(d == 1)
    def _():
        run(True)

    if want_final:
        @pl.when(tb == ntb - 1)
        def _():
            for gi in range(ngrp):
                for h in range(nh):
                    sf_ref[0, 0, gi * nh + h] = st_s[gi, h * dv:(h + 1) * dv, h * dk:(h + 1) * dk].T


def _gated_scan(kind, inputs, init, row0, batch, t_len, heads, dk, dv, ngrp, nh, want_final):
    chunk = min(GATED_CHUNK[kind], SCAN_TIME_BLOCK, t_len)
    tbl = _pick(t_len, SCAN_TIME_BLOCK, chunk)
    ntb = t_len // tbl
    rb0 = row0 // tbl
    hpb = ngrp * nh
    ng = heads // hpb
    kw, vw = hpb * dk, hpb * dv
    masks = _pair_masks(chunk, nh)
    mask_spec = pl.BlockSpec((1,) + masks.shape[1:], lambda b, g, d, tb: (d, 0, 0))

    def part_rows(b, g, d, tb):
        return b * ntb + _time_block(d, tb, ntb)

    def full_rows(b, g, d, tb):
        return rb0 + part_rows(b, g, d, tb)

    if kind == "hgrn":
        qi, z, f_b, lb = inputs
        f_col0 = (heads * dk + heads * dv) // kw
        in_specs = [
            pl.BlockSpec((tbl, kw), lambda b, g, d, tb: (part_rows(b, g, d, tb), g)),
            pl.BlockSpec((tbl, vw), lambda b, g, d, tb: (part_rows(b, g, d, tb), (heads * dk) // vw + g)),
            pl.BlockSpec((tbl, kw), lambda b, g, d, tb: (full_rows(b, g, d, tb), f_col0 + d * ng + g)),
            pl.BlockSpec((1, 1, kw), lambda b, g, d, tb: (d, 0, g)),
            pl.BlockSpec((1, 1, kw), lambda b, g, d, tb: (d, 0, g)),
        ]
        args = [qi, qi, z, f_b, lb]
    else:
        qk, z, a_low, w_a2, b_a = inputs
        rank = a_low.shape[-1]
        in_specs = [
            pl.BlockSpec((tbl, kw), lambda b, g, d, tb: (part_rows(b, g, d, tb), g)),
            pl.BlockSpec((tbl, kw), lambda b, g, d, tb: (part_rows(b, g, d, tb), ng + g)),
            pl.BlockSpec((tbl, vw), lambda b, g, d, tb: (full_rows(b, g, d, tb), (2 * heads * dk) // vw + g)),
            pl.BlockSpec((1, tbl, rank), lambda b, g, d, tb: (d, full_rows(b, g, d, tb), 0)),
            pl.BlockSpec((1, rank, kw), lambda b, g, d, tb: (d, 0, g)),
            pl.BlockSpec((1, 1, kw), lambda b, g, d, tb: (d, 0, g)),
        ]
        args = [qk, qk, z, a_low, w_a2, b_a]
    in_specs.append(mask_spec)
    args.append(masks)
    state_spec = pl.BlockSpec((1, 1, hpb, dk, dv), lambda b, g, d, tb: (b, d, g, 0, 0))
    if init is not None:
        in_specs.append(state_spec)
        args.append(init)
    out_shape = [jax.ShapeDtypeStruct((2, batch * t_len, heads * dv), F32)]
    out_specs = [pl.BlockSpec((1, tbl, vw), lambda b, g, d, tb: (d, part_rows(b, g, d, tb), g))]
    if want_final:
        out_shape.append(jax.ShapeDtypeStruct((batch, 2, heads, dk, dv), F32))
        out_specs.append(state_spec)
    scratch = [pltpu.VMEM((ngrp, nh * dv, nh * dk), F32), pltpu.VMEM((tbl, kw), F32)]
    if kind == "hgrn":
        scratch.append(pltpu.VMEM((tbl, kw), F32))
    res = pl.pallas_call(
        functools.partial(_gated_body, kind, init is not None, want_final, ntb, ngrp, nh, dk, dv, chunk),
        out_shape=out_shape,
        grid=(batch, ng, 2, ntb),
        in_specs=in_specs,
        out_specs=out_specs,
        scratch_shapes=scratch,
        compiler_params=_params("parallel", "parallel", "arbitrary", "arbitrary"),
        name=kind + "_scan",
    )(*args)
    return (res[0], res[1]) if want_final else (res[0], None)


def _head_norm_body(gate_kind, heads, dv, o_ref, z_ref, g_ref, y_ref):
    for h in range(heads):
        sl = slice(h * dv, (h + 1) * dv)
        o = o_ref[0, :, sl] + o_ref[1, :, sl]
        o = o * lax.rsqrt(jnp.mean(o * o, axis=-1, keepdims=True) + EPS) * g_ref[:, sl]
        zg = z_ref[:, sl]
        gate = _sigmoid(zg) if gate_kind == "sigmoid" else _silu(zg)
        y_ref[:, sl] = (gate * o).astype(y_ref.dtype)


def _head_norm_gate(out2, z, gate_col0, norm_g, row0, gate_kind):
    _, n, hv = out2.shape
    heads, dv = norm_g.shape
    bm = _pick(n, ROW_BLOCK, 8)
    rb0 = row0 // bm
    return pl.pallas_call(
        functools.partial(_head_norm_body, gate_kind, heads, dv),
        out_shape=jax.ShapeDtypeStruct((n, hv), BF16),
        grid=(n // bm,),
        in_specs=[pl.BlockSpec((2, bm, hv), lambda i: (0, i, 0)),
                  pl.BlockSpec((bm, hv), lambda i: (rb0 + i, gate_col0 // hv)),
                  pl.BlockSpec((1, hv), lambda i: (0, 0))],
        out_specs=pl.BlockSpec((bm, hv), lambda i: (i, 0)),
        compiler_params=_params("parallel"),
        name="head_norm_gate",
    )(out2, z, norm_g.reshape(1, hv))


def kernel(x_prompt, x_sample, c, c_ctx, state_mlstm_C, state_mlstm_n, state_mlstm_m, state_hgrn_S,
           state_gla_S, mod_w, mod_b, norm_g, ffn_w_gate, ffn_w_up, ffn_w_down, mlstm_w_in, mlstm_conv,
           mlstm_gate_b, mlstm_norm_g, mlstm_w_out, hgrn_w_in, hgrn_conv, hgrn_f_b, hgrn_lb_logits,
           hgrn_norm_g, hgrn_w_out, gla_w_in, gla_conv, gla_w_a2, gla_b_a, gla_norm_g, gla_w_out,
           final_norm_g):
    bp, tp, d = x_prompt.shape
    bs, ts, _ = x_sample.shape
    n_p, n_s = bp * tp, bs * ts
    depth = mod_w.shape[0]
    a_heads, a_dk, a_dv = state_mlstm_C.shape[3:]
    b_heads, b_dk, b_dv = state_hgrn_S.shape[3:]
    c_heads, c_dk, c_dv = state_gla_S.shape[3:]
    assert 1 + bs <= COND_ROWS and n_p % ts == 0

    x = jnp.concatenate([x_prompt.reshape(n_p, d), x_sample.reshape(n_s, d)], axis=0)
    cond = jnp.zeros((COND_ROWS, d), F32).at[0].set(c_ctx).at[1:1 + bs].set(c)
    mod = _modulation(cond, mod_w, mod_b).reshape(depth, COND_ROWS, N_MOD, d)
    cond_of = _CondRows(n_p, ts)
    parts = ((0, bp, tp, 1, True), (n_p, bs, ts, ts // GRID_W, False))

    lb_p = jax.nn.softmax(hgrn_lb_logits.astype(F32), axis=1)
    lb_all = jnp.cumsum(lb_p, axis=1) - lb_p[:, :1]

    def swiglu(x, l, s, sub):
        h = _norm_mod(x, norm_g[l, sub], mod[l], sub, cond_of)
        a = _swiglu_up(h, ffn_w_gate[l, s].astype(BF16), ffn_w_up[l, s].astype(BF16))
        return _matmul_residual(a, ffn_w_down[l, s].astype(BF16), x, mod[l], 3 * sub + 2, 0.5, cond_of)

    def mlstm(h, j):
        nq, nv = a_heads * a_dk, a_heads * a_dv
        w_in = mlstm_w_in[j]
        z = _matmul(h, w_in[:, :2 * nq + 2 * nv].astype(BF16), F32)
        n_tail = w_in.shape[1] - (2 * nq + 2 * nv)
        w_tail = jnp.pad(w_in[:, 2 * nq + 2 * nv:], ((0, 0), (0, 128 - n_tail))).astype(BF16)
        zt = _matmul(h, w_tail, F32, bn_pref=128)[:, :n_tail]
        gates = zt.reshape(-1, 2, 2, a_heads).transpose(1, 3, 0, 2)
        gate_b = mlstm_gate_b[j].astype(F32).transpose(0, 2, 1).reshape(2, a_heads, 1, 2)
        ys, finals = [], None
        for row0, batch, t_len, rows, is_ctx in parts:
            qk = _conv_silu(z, mlstm_conv[j], 2 * nq, row0, batch, t_len, rows)
            init = None if is_ctx else (state_mlstm_C[:, j], state_mlstm_n[:, j], state_mlstm_m[:, j])
            out2, fin = _mlstm_scan(qk, z, gates, gate_b, init, row0, batch, t_len, a_heads, a_dk, a_dv,
                                    2 if a_heads % 2 == 0 else 1, is_ctx)
            ys.append(_head_norm_gate(out2, z, 2 * nq + nv, mlstm_norm_g[j], row0, "sigmoid"))
            finals = fin if is_ctx else finals
        return jnp.concatenate(ys, axis=0), mlstm_w_out[j], finals

    def hgrn(h, j, l):
        nq, ni = b_heads * b_dk, b_heads * b_dv
        z = _matmul(h, hgrn_w_in[j].astype(BF16), F32)
        f_b = hgrn_f_b[j].astype(F32).reshape(2, 1, nq)
        lb = lb_all[:, l].reshape(2, 1, nq)
        nh = 2 if b_heads % 2 == 0 else 1
        ngrp = 2 if b_heads % (2 * nh) == 0 else 1
        ys, finals = [], None
        for row0, batch, t_len, rows, is_ctx in parts:
            qi = _conv_silu(z, hgrn_conv[j], nq + ni, row0, batch, t_len, rows)
            init = None if is_ctx else state_hgrn_S[:, j]
            out2, fin = _gated_scan("hgrn", (qi, z, f_b, lb), init, row0, batch, t_len,
                                    b_heads, b_dk, b_dv, ngrp, nh, is_ctx)
            ys.append(_head_norm_gate(out2, z, 3 * nq + ni, hgrn_norm_g[j], row0, "silu"))
            finals = (fin,) if is_ctx else finals
        return jnp.concatenate(ys, axis=0), hgrn_w_out[j], finals

    def gla(h, j):
        nq, nv = c_heads * c_dk, c_heads * c_dv
        w_in = gla_w_in[j]
        z = _matmul(h, w_in[:, :2 * nq + 2 * nv].astype(BF16), F32)
        n_tail = w_in.shape[1] - (2 * nq + 2 * nv)
        w_tail = jnp.pad(w_in[:, 2 * nq + 2 * nv:], ((0, 0), (0, 128 - n_tail))).astype(BF16)
        zt = _matmul(h, w_tail, F32, bn_pref=128)[:, :n_tail]
        a_low = zt.reshape(-1, 2, n_tail // 2).transpose(1, 0, 2)
        b_a = gla_b_a[j].astype(F32).reshape(2, 1, nq)
        ys, finals = [], None
        for row0, batch, t_len, rows, is_ctx in parts:
            qk = _conv_silu(z, gla_conv[j], 2 * nq, row0, batch, t_len, rows)
            init = None if is_ctx else state_gla_S[:, j]
            out2, fin = _gated_scan("gla", (qk, z, a_low, gla_w_a2[j], b_a), init, row0, batch, t_len,
                                    c_heads, c_dk, c_dv, 1, 1, is_ctx)
            ys.append(_head_norm_gate(out2, z, 2 * nq + nv, gla_norm_g[j], row0, "silu"))
            finals = (fin,) if is_ctx else finals
        return jnp.concatenate(ys, axis=0), gla_w_out[j], finals

    states = []
    for l in range(depth):
        x = swiglu(x, l, 0, 0)
        h = _norm_mod(x, norm_g[l, 1], mod[l], 1, cond_of)
        j, kind = l // 3, l % 3
        y, w_out, fin = mlstm(h, j) if kind == 0 else hgrn(h, j, l) if kind == 1 else gla(h, j)
        x = _matmul_residual(y, w_out.astype(BF16), x, mod[l], 5, 1.0, cond_of)
        x = swiglu(x, l, 1, 2)
        states.append(fin)

    y = _final_norm(x, final_norm_g)
    y_prompt = y[:n_p].reshape(bp, tp, d)
    y_sample = y[n_p:].reshape(bs, ts, d)
    new_c = jnp.stack([states[l][0] for l in range(0, depth, 3)], axis=1)
    new_n = jnp.stack([states[l][1] for l in range(0, depth, 3)], axis=1)
    new_m = jnp.stack([states[l][2] for l in range(0, depth, 3)], axis=1)
    new_hgrn = jnp.stack([states[l][0] for l in range(1, depth, 3)], axis=1)
    new_gla = jnp.stack([states[l][0] for l in range(2, depth, 3)], axis=1)
    return (y_prompt, y_sample, new_c, new_n, new_m, new_hgrn, new_gla)
```

```python
import functools
import math

import jax
import jax.numpy as jnp
import numpy as np
from jax import lax
from jax.experimental import pallas as pl
from jax.experimental.pallas import tpu as pltpu

F32 = jnp.float32
BF16 = jnp.bfloat16

GRID_W = 64
EPS = 1e-6
GLA_TAU = 16.0
MLSTM_CHUNK = 64
GATED_CHUNK = {"hgrn": 128, "gla": 128}
N_MOD = 9
COND_ROWS = 8

VMEM_LIMIT = 56 * 1024 * 1024
SCAN_TIME_BLOCK = 512
ROW_BLOCK = 256


def _params(*sem):
    return pltpu.CompilerParams(dimension_semantics=sem, vmem_limit_bytes=VMEM_LIMIT)


def _pick(n, pref, align):
    if n <= pref:
        return n
    best = None
    for cand in range(align, pref + 1, align):
        if n % cand == 0:
            best = cand
    assert best is not None, (n, pref, align)
    return best


def _sigmoid(x):
    return 1.0 / (1.0 + jnp.exp(-x))


def _silu(x):
    return x * _sigmoid(x)


def _log_sigmoid(x):
    return jnp.minimum(x, 0.0) - jnp.log1p(jnp.exp(-jnp.abs(x)))


class _CondRows:
    def __init__(self, n_prompt_rows, t_sample):
        assert n_prompt_rows % t_sample == 0
        self.n_prompt_rows, self.t_sample = n_prompt_rows, t_sample

    def block_rows(self, pref):
        return _pick(self.t_sample, pref, 8)

    def index(self, bm):
        assert self.t_sample % bm == 0

        def f(i):
            start = i * bm
            return jnp.where(start < self.n_prompt_rows, 0, 1 + (start - self.n_prompt_rows) // self.t_sample)
        return f


def _mod_body(c_ref, w_ref, b_ref, o_ref):
    a = _silu(c_ref[...]).astype(BF16)
    o_ref[0] = jnp.dot(a, w_ref[0].astype(BF16), preferred_element_type=F32) + b_ref[0]


def _modulation(cond, mod_w, mod_b):
    depth, d, n = mod_w.shape
    bn = _pick(n, 512, 128)
    return pl.pallas_call(
        _mod_body,
        out_shape=jax.ShapeDtypeStruct((depth, COND_ROWS, n), F32),
        grid=(depth, n // bn),
        in_specs=[pl.BlockSpec((COND_ROWS, d), lambda l, j: (0, 0)),
                  pl.BlockSpec((1, d, bn), lambda l, j: (l, 0, j)),
                  pl.BlockSpec((1, 1, bn), lambda l, j: (l, 0, j))],
        out_specs=pl.BlockSpec((1, COND_ROWS, bn), lambda l, j: (l, 0, j)),
        compiler_params=_params("arbitrary", "arbitrary"),
        name="modulation",
    )(cond, mod_w, mod_b.reshape(depth, 1, n))


def _norm_mod_body(sub, x_ref, g_ref, m_ref, o_ref):
    x = x_ref[...]
    y = x * lax.rsqrt(jnp.mean(x * x, axis=-1, keepdims=True) + EPS) * g_ref[...]
    m = m_ref[0]
    shift = m[3 * sub:3 * sub + 1, :]
    scale = m[3 * sub + 1:3 * sub + 2, :]
    o_ref[...] = (y * (1.0 + scale) + shift).astype(o_ref.dtype)


def _norm_mod(x, g, mod_l, sub, cond_of):
    n, d = x.shape
    bm = cond_of.block_rows(ROW_BLOCK)
    cr = cond_of.index(bm)
    return pl.pallas_call(
        functools.partial(_norm_mod_body, sub),
        out_shape=jax.ShapeDtypeStruct((n, d), BF16),
        grid=(n // bm,),
        in_specs=[pl.BlockSpec((bm, d), lambda i: (i, 0)),
                  pl.BlockSpec((1, d), lambda i: (0, 0)),
                  pl.BlockSpec((1, N_MOD, d), lambda i: (cr(i), 0, 0))],
        out_specs=pl.BlockSpec((bm, d), lambda i: (i, 0)),
        compiler_params=_params("parallel"),
        name="norm_mod",
    )(x, g.reshape(1, d), mod_l)


def _final_norm_body(x_ref, g_ref, o_ref):
    x = x_ref[...]
    o_ref[...] = x * lax.rsqrt(jnp.mean(x * x, axis=-1, keepdims=True) + EPS) * g_ref[...]


def _final_norm(x, g):
    n, d = x.shape
    bm = _pick(n, ROW_BLOCK, 8)
    return pl.pallas_call(
        _final_norm_body,
        out_shape=jax.ShapeDtypeStruct((n, d), F32),
        grid=(n // bm,),
        in_specs=[pl.BlockSpec((bm, d), lambda i: (i, 0)), pl.BlockSpec((1, d), lambda i: (0, 0))],
        out_specs=pl.BlockSpec((bm, d), lambda i: (i, 0)),
        compiler_params=_params("parallel"),
        name="final_norm",
    )(x, g.reshape(1, d))


def _matmul_body(x_ref, w_ref, o_ref):
    o_ref[...] = jnp.dot(x_ref[...], w_ref[...], preferred_element_type=F32).astype(o_ref.dtype)


def _matmul(x, w, out_dtype, bm_pref=1024, bn_pref=512):
    n, k = x.shape
    p = w.shape[1]
    bm = _pick(n, bm_pref, 8)
    bn = _pick(p, bn_pref, 128)
    return pl.pallas_call(
        _matmul_body,
        out_shape=jax.ShapeDtypeStruct((n, p), out_dtype),
        grid=(n // bm, p // bn),
        in_specs=[pl.BlockSpec((bm, k), lambda i, j: (i, 0)),
                  pl.BlockSpec((k, bn), lambda i, j: (0, j))],
        out_specs=pl.BlockSpec((bm, bn), lambda i, j: (i, j)),
        compiler_params=_params("parallel", "arbitrary"),
        name="matmul",
    )(x, w)


def _swiglu_up_body(h_ref, wg_ref, wu_ref, o_ref):
    h = h_ref[...]
    g = jnp.dot(h, wg_ref[...], preferred_element_type=F32)
    u = jnp.dot(h, wu_ref[...], preferred_element_type=F32)
    o_ref[...] = (_silu(g) * u).astype(o_ref.dtype)


def _swiglu_up(h, wg, wu, wi, bm_pref=1024, bn_pref=256):
    n, k = h.shape
    f = wg.shape[2]
    bm = _pick(n, bm_pref, 8)
    bn = _pick(f, bn_pref, 128)
    return pl.pallas_call(
        _swiglu_up_body,
        out_shape=jax.ShapeDtypeStruct((n, f), BF16),
        grid=(n // bm, f // bn),
        in_specs=[pl.BlockSpec((bm, k), lambda i, j: (i, 0)),
                  pl.BlockSpec((pl.Squeezed(), k, bn), lambda i, j: (wi, 0, j)),
                  pl.BlockSpec((pl.Squeezed(), k, bn), lambda i, j: (wi, 0, j))],
        out_specs=pl.BlockSpec((bm, bn), lambda i, j: (i, j)),
        compiler_params=_params("parallel", "arbitrary"),
        name="swiglu_up",
    )(h, wg, wu)


def _matmul_residual_body(gate_row, coef, a_ref, w_ref, x_ref, m_ref, o_ref):
    y = jnp.dot(a_ref[...], w_ref[...], preferred_element_type=F32)
    gate = m_ref[0][gate_row:gate_row + 1, :]
    o_ref[...] = x_ref[...] + (coef * gate) * y


def _matmul_residual(a, w, wi, x, mod_l, gate_row, coef, cond_of, bm_pref=512, bn_pref=512):
    n, k = a.shape
    d = w.shape[2]
    bm = cond_of.block_rows(bm_pref)
    bn = _pick(d, bn_pref, 128)
    cr = cond_of.index(bm)
    return pl.pallas_call(
        functools.partial(_matmul_residual_body, gate_row, coef),
        out_shape=jax.ShapeDtypeStruct((n, d), F32),
        grid=(n // bm, d // bn),
        in_specs=[pl.BlockSpec((bm, k), lambda i, j: (i, 0)),
                  pl.BlockSpec((pl.Squeezed(), k, bn), lambda i, j: (wi, 0, j)),
                  pl.BlockSpec((bm, bn), lambda i, j: (i, j)),
                  pl.BlockSpec((1, N_MOD, bn), lambda i, j: (cr(i), 0, j))],
        out_specs=pl.BlockSpec((bm, bn), lambda i, j: (i, j)),
        compiler_params=_params("parallel", "arbitrary"),
        name="matmul_residual",
    )(a, w, x, mod_l)


def _conv_body(rows, cols, z_ref, w_ref, o_ref):
    u = z_ref[...]
    t_len = u.shape[0]
    t = lax.broadcasted_iota(jnp.int32, (t_len, 1), 0)
    if rows == 1:
        r, c = jnp.zeros_like(t), t
    else:
        r, c = t >> (cols.bit_length() - 1), t & (cols - 1)
    w = w_ref[...]
    taps = (jnp.where(c >= 1, pltpu.roll(u, 1, 0), 0.0), u, jnp.where(c < cols - 1, pltpu.roll(u, t_len - 1, 0), 0.0))
    acc = None
    for i in range(3):
        if rows == 1 and i != 1:
            continue
        part = sum(taps[j] * w[3 * i + j:3 * i + j + 1, :] for j in range(3))
        if i != 1:
            part = jnp.where((r + (i - 1) >= 0) & (r + (i - 1) < rows),
                             pltpu.roll(part, (-(i - 1) * cols) % t_len, 0), 0.0)
        acc = part if acc is None else acc + part
    o_ref[...] = _silu(acc)


def _conv_silu(z, conv_w, n_conv, row0, batch, t_len, rows):
    cols = t_len // rows
    assert rows == 1 or (cols & (cols - 1)) == 0
    cb = _pick(n_conv, 256, 128)
    rb0 = row0 // t_len
    return pl.pallas_call(
        functools.partial(_conv_body, rows, cols),
        out_shape=jax.ShapeDtypeStruct((batch * t_len, n_conv), F32),
        grid=(batch, n_conv // cb),
        in_specs=[pl.BlockSpec((t_len, cb), lambda b, j: (rb0 + b, j)),
                  pl.BlockSpec((9, cb), lambda b, j: (0, j))],
        out_specs=pl.BlockSpec((t_len, cb), lambda b, j: (b, j)),
        compiler_params=_params("parallel", "arbitrary"),
        name="conv_silu",
    )(z, conv_w.reshape(9, n_conv))


def _time_block(d, tb, ntb):
    return tb + d * (ntb - 1 - 2 * tb)


def _dir_sign(d):
    return 1 - 2 * d


def _mlstm_body(has_init, want_final, ntb, hpb, dk, dv, *refs):
    it = iter(refs)
    q_ref, k_ref, v_ref, g_ref, gb_ref = (next(it) for _ in range(5))
    if has_init:
        c0_ref, n0_ref, m0_ref = (next(it) for _ in range(3))
    o_ref = next(it)
    if want_final:
        cf_ref, nf_ref, mf_ref = (next(it) for _ in range(3))
    c_s, n_s, m_s = (next(it) for _ in range(3))

    d = pl.program_id(2)
    tb = pl.program_id(3)
    L = MLSTM_CHUNK
    tbl = q_ref.shape[0]
    q_scale = dk ** -0.5

    @pl.when(tb == 0)
    def _():
        if has_init:
            c_s[...] = c0_ref[0, 0]
            n_s[...] = n0_ref[0, 0]
            m_s[...] = m0_ref[0, 0]
        else:
            c_s[...] = jnp.zeros_like(c_s)
            n_s[...] = jnp.zeros_like(n_s)
            m_s[...] = jnp.zeros_like(m_s)

    sign = _dir_sign(d)
    sub = lax.broadcasted_iota(jnp.int32, (L, L), 0)
    lane = lax.broadcasted_iota(jnp.int32, (L, L), 1)
    sees = sign * (sub - lane) >= 0
    sees_t = sign * (lane - sub) >= 0
    eye = sub == lane
    ncl = tbl // L

    def chunk(ci, carry):
        lc = ci + d * (ncl - 1 - 2 * ci)
        r0 = pl.multiple_of(lc * L, L)
        rows = pl.ds(r0, L)
        loaded = []
        for hh in range(hpb):
            ks = slice(hh * dk, (hh + 1) * dk)
            vs = slice(hh * dv, (hh + 1) * dv)
            loaded.append((q_ref[rows, ks], k_ref[rows, ks], v_ref[rows, vs], g_ref[0, hh, rows, :] + gb_ref[0, hh],
                           c_s[hh], n_s[hh], m_s[hh]))
        hs = range(hpb)
        q_raw, kc, v_raw, gates, c_st, n_st, m_st = zip(*loaded)
        qc = [(q_raw[h] * q_scale).astype(BF16) for h in hs]
        vc = [v_raw[h].astype(BF16) for h in hs]
        i_col = [gates[h][:, 0:1] for h in hs]
        f_col = [_log_sigmoid(gates[h][:, 1:2]) for h in hs]
        qk = [_dot_nt(qc[h], kc[h].astype(BF16)) for h in hs]
        qc_c = [jnp.dot(qc[h], c_st[h].astype(BF16), preferred_element_type=F32) for h in hs]
        b_row = [jnp.sum(jnp.where(sees_t, f_col[h], 0.0), axis=0, keepdims=True) for h in hs]
        i_row = [jnp.sum(jnp.where(eye, i_col[h], 0.0), axis=0, keepdims=True) for h in hs]
        b_end = [jnp.sum(f_col[h], axis=0, keepdims=True) for h in hs]
        b_col = [jnp.sum(jnp.where(eye, b_row[h], 0.0), axis=1, keepdims=True) for h in hs]
        dmat = [jnp.where(sees, b_col[h] - b_row[h] + i_row[h], -jnp.inf) for h in hs]
        w = [b_end[h] - b_col[h] + i_col[h] for h in hs]
        d_max = [jnp.max(dmat[h], axis=1, keepdims=True) for h in hs]
        w_max = [jnp.max(w[h], axis=0, keepdims=True) for h in hs]
        m_inter = [b_col[h] + m_st[h] for h in hs]
        m_t = [jnp.maximum(m_inter[h], d_max[h]) for h in hs]
        m_new = [jnp.maximum(b_end[h] + m_st[h], w_max[h]) for h in hs]
        s = [qk[h] * jnp.exp(dmat[h] - m_t[h]) for h in hs]
        e_inter = [jnp.exp(m_inter[h] - m_t[h]) for h in hs]
        kw = [kc[h] * jnp.exp(w[h] - m_new[h]) for h in hs]
        decay = [jnp.exp(b_end[h] + m_st[h] - m_new[h]) for h in hs]
        num = [jnp.dot(s[h].astype(BF16), vc[h], preferred_element_type=F32) + e_inter[h] * qc_c[h] for h in hs]
        kv = [_dot_tn(kw[h].astype(BF16), vc[h]) for h in hs]
        qn = [jnp.sum(qc[h].astype(F32) * n_st[h].astype(BF16).astype(F32), axis=1, keepdims=True) for h in hs]
        den = [jnp.sum(s[h], axis=1, keepdims=True) + e_inter[h] * qn[h] for h in hs]
        results = [(num[h] / jnp.maximum(jnp.abs(den[h]), jnp.exp(-m_t[h])), decay[h] * c_st[h] + kv[h],
                    decay[h] * n_st[h] + jnp.sum(kw[h], axis=0, keepdims=True), m_new[h]) for h in hs]
        for hh, (out, c_new, n_new, m_new) in enumerate(results):
            o_ref[0, rows, hh * dv:(hh + 1) * dv] = out
            c_s[hh] = c_new
            n_s[hh] = n_new
            m_s[hh] = m_new
        return carry

    lax.fori_loop(0, ncl, chunk, 0)

    if want_final:
        @pl.when(tb == ntb - 1)
        def _():
            cf_ref[0, 0] = c_s[...]
            nf_ref[0, 0] = n_s[...]
            mf_ref[0, 0] = m_s[...]


def _mlstm_scan(qk, z, gates, gate_b, init, row0, batch, t_len, heads, dk, dv, hpb, want_final):
    tbl = _pick(t_len, SCAN_TIME_BLOCK, MLSTM_CHUNK)
    ntb = t_len // tbl
    rb0 = row0 // tbl
    ng = heads // hpb
    v_col0 = (2 * heads * dk) // (hpb * dv)

    def part_rows(b, h, d, tb):
        return b * ntb + _time_block(d, tb, ntb)

    in_specs = [
        pl.BlockSpec((tbl, hpb * dk), lambda b, h, d, tb: (part_rows(b, h, d, tb), h)),
        pl.BlockSpec((tbl, hpb * dk), lambda b, h, d, tb: (part_rows(b, h, d, tb), ng + h)),
        pl.BlockSpec((tbl, hpb * dv), lambda b, h, d, tb: (rb0 + part_rows(b, h, d, tb), v_col0 + h)),
        pl.BlockSpec((1, hpb, tbl, 2), lambda b, h, d, tb: (d, h, rb0 + part_rows(b, h, d, tb), 0)),
        pl.BlockSpec((1, hpb, 1, 2), lambda b, h, d, tb: (d, h, 0, 0)),
    ]
    args = [qk, qk, z, gates, gate_b]
    state_specs = [
        pl.BlockSpec((1, 1, hpb, dk, dv), lambda b, h, d, tb: (b, d, h, 0, 0)),
        pl.BlockSpec((1, 1, hpb, 1, dk), lambda b, h, d, tb: (b, d, h, 0, 0)),
        pl.BlockSpec((1, 1, hpb, 1, 1), lambda b, h, d, tb: (b, d, h, 0, 0)),
    ]
    if init is not None:
        c0, n0, m0 = init
        in_specs += state_specs
        args += [c0, n0.reshape(batch, 2, heads, 1, dk), m0.reshape(batch, 2, heads, 1, 1)]
    out_shape = [jax.ShapeDtypeStruct((2, batch * t_len, heads * dv), F32)]
    out_specs = [pl.BlockSpec((1, tbl, hpb * dv), lambda b, h, d, tb: (d, part_rows(b, h, d, tb), h))]
    if want_final:
        out_shape += [jax.ShapeDtypeStruct((batch, 2, heads, dk, dv), F32),
                      jax.ShapeDtypeStruct((batch, 2, heads, 1, dk), F32),
                      jax.ShapeDtypeStruct((batch, 2, heads, 1, 1), F32)]
        out_specs += state_specs
    res = pl.pallas_call(
        functools.partial(_mlstm_body, init is not None, want_final, ntb, hpb, dk, dv),
        out_shape=out_shape,
        grid=(batch, ng, 2, ntb),
        in_specs=in_specs,
        out_specs=out_specs,
        scratch_shapes=[pltpu.VMEM((hpb, dk, dv), F32), pltpu.VMEM((hpb, 1, dk), F32),
                        pltpu.VMEM((hpb, 1, 1), F32)],
        compiler_params=_params("parallel", "parallel", "arbitrary", "arbitrary"),
        name="mlstm_scan",
    )(*args)
    if want_final:
        out, cf, nf, mf = res
        return out, (cf, nf.reshape(batch, 2, heads, dk), mf.reshape(batch, 2, heads))
    return res[0], None


def _pair_masks(L, nh):
    p = L.bit_length() - 1
    t = np.arange(L)[:, None]
    s = np.arange(L)[None, :]
    out = np.zeros((2, p + 1, L, L), np.float32)
    for j in range(p):
        same = (t >> (j + 1)) == (s >> (j + 1))
        t_hi, s_hi = ((t >> j) & 1) == 1, ((s >> j) & 1) == 1
        out[0, j] = same & t_hi & ~s_hi
        out[1, j] = same & ~t_hi & s_hi
    out[:, p] = np.eye(L, dtype=np.float32)
    return jnp.asarray(np.tile(out, (1, 1, 1, nh)).reshape(2, (p + 1) * L, nh * L))


def _block_rows(x, blk, r):
    n, w = x.shape
    if blk >= 16:
        return jnp.concatenate([jnp.broadcast_to(x[b0 + r:b0 + r + 1, :], (blk, w)) for b0 in range(0, n, blk)],
                               axis=0)
    x3 = x.reshape(n // 8, 8, w)
    sub = lax.broadcasted_iota(jnp.int32, (1, 8, 1), 1)
    out = None
    for s0 in range(0, 8, blk):
        bc = jnp.broadcast_to(x3[:, s0 + r:s0 + r + 1, :], x3.shape)
        out = bc if out is None else jnp.where(sub >= s0, bc, out)
    return out.reshape(n, w)


def _head_diag_rows(x, nh, w):
    if nh == 1:
        return x
    lane_head = lax.broadcasted_iota(jnp.int32, (1, nh * w), 1) // w
    return jnp.concatenate([jnp.where(lane_head == h, x, jnp.zeros_like(x)) for h in range(nh)], axis=0)


def _dot_nt(a, b):
    return lax.dot_general(a, b, (((1,), (1,)), ((), ())), preferred_element_type=F32)


def _dot_tn(a, b):
    return lax.dot_general(a, b, (((0,), (0,)), ((), ())), preferred_element_type=F32)


def _gated_body(kind, has_init, want_final, ntb, ngrp, nh, dk, dv, L, *refs):
    it = iter(refs)
    if kind == "hgrn":
        q_ref, v_ref, fpre_ref, fb_ref, lb_ref = (next(it) for _ in range(5))
    else:
        q_ref, k_ref, v_ref, alow_ref, wa_ref, ba_ref = (next(it) for _ in range(6))
    m_ref = next(it)
    if has_init:
        s0_ref = next(it)
    o_ref = next(it)
    if want_final:
        sf_ref = next(it)
    st_s, lg_s = next(it), next(it)
    if kind == "hgrn":
        kk_s = next(it)

    d = pl.program_id(2)
    tb = pl.program_id(3)
    tbl = q_ref.shape[0]
    p = L.bit_length() - 1
    kw, vw = nh * dk, nh * dv
    q_scale = 1.0 if kind == "hgrn" else dk ** -0.5

    @pl.when(tb == 0)
    def _():
        st_s[...] = jnp.zeros_like(st_s)
        if has_init:
            for gi in range(ngrp):
                for h in range(nh):
                    st_s[gi, h * dv:(h + 1) * dv, h * dk:(h + 1) * dk] = s0_ref[0, 0, gi * nh + h].T

    if kind == "hgrn":
        lb = lb_ref[0]
        f = lb + (1.0 - lb) * _sigmoid(fpre_ref[...] + fb_ref[0])
        lg_s[...] = jnp.log(f)
        kk_s[...] = 1.0 - f
    else:
        pre = jnp.dot(alow_ref[0].astype(BF16), wa_ref[0].astype(BF16), preferred_element_type=F32)
        lg_s[...] = _log_sigmoid(pre + ba_ref[0]) / GLA_TAU

    row = lax.broadcasted_iota(jnp.int32, (L, 1), 0)
    if nh > 1:
        own = ((lax.broadcasted_iota(jnp.int32, (vw, kw), 0) // dv)
               == (lax.broadcasted_iota(jnp.int32, (vw, kw), 1) // dk))
    ncl = tbl // L

    def run(rev):
        def chunk(ci, carry):
            r0 = pl.multiple_of(((ncl - 1 - ci) if rev else ci) * L, L)
            rows = pl.ds(r0, L)
            loaded = []
            for gi in range(ngrp):
                ks = slice(gi * kw, (gi + 1) * kw)
                vs = slice(gi * vw, (gi + 1) * vw)
                loaded.append((q_ref[rows, ks], kk_s[rows, ks] if kind == "hgrn" else k_ref[rows, ks],
                               v_ref[rows, vs], lg_s[rows, ks], st_s[gi]))
            results = []
            for q, k, v, seg, st in loaded:
                q = q * q_scale
                v = v.astype(BF16)
                a = m_ref[0, p * L:(p + 1) * L, :] * _dot_nt(q.astype(BF16), _head_diag_rows(k.astype(BF16), nh, dk))
                for j in range(p):
                    half = 1 << j
                    edge = _block_rows(seg, 2 * half, half if rev else half - 1)
                    later = (((row >> j) & 1) == 1) != rev
                    z = (jnp.exp(jnp.where(later, seg, edge - seg)) * jnp.where(later, q, k)).astype(BF16)
                    a = a + m_ref[0, j * L:(j + 1) * L, :] * _dot_nt(z, _head_diag_rows(z, nh, dk))
                    seg = seg + jnp.where(later, edge, 0.0)
                total = seg[0:1, :] if rev else seg[L - 1:L, :]
                out = (jnp.dot(a.astype(BF16), _head_diag_rows(v, nh, dv), preferred_element_type=F32)
                       + _dot_nt((q * jnp.exp(seg)).astype(BF16), st.astype(BF16)))
                st = st * jnp.exp(total) + _dot_tn(v, (k * jnp.exp(total - seg)).astype(BF16))
                results.append((out, st if nh == 1 else jnp.where(own, st, 0.0)))
            for gi, (out, st) in enumerate(results):
                o_ref[0, rows, gi * vw:(gi + 1) * vw] = out
                st_s[gi] = st
            return carry

        lax.fori_loop(0, ncl, chunk, 0)

    @pl.when(d == 0)
    def _():
        run(False)

    @pl.when(d == 1)
    def _():
        run(True)

    if want_final:
        @pl.when(tb == ntb - 1)
        def _():
            for gi in range(ngrp):
                for h in range(nh):
                    sf_ref[0, 0, gi * nh + h] = st_s[gi, h * dv:(h + 1) * dv, h * dk:(h + 1) * dk].T


def _gated_scan(kind, inputs, init, row0, batch, t_len, heads, dk, dv, ngrp, nh, want_final):
    chunk = min(GATED_CHUNK[kind], SCAN_TIME_BLOCK, t_len)
    tbl = _pick(t_len, SCAN_TIME_BLOCK, chunk)
    ntb = t_len // tbl
    rb0 = row0 // tbl
    hpb = ngrp * nh
    ng = heads // hpb
    kw, vw = hpb * dk, hpb * dv
    masks = _pair_masks(chunk, nh)
    mask_spec = pl.BlockSpec((1,) + masks.shape[1:], lambda b, g, d, tb: (d, 0, 0))

    def part_rows(b, g, d, tb):
        return b * ntb + _time_block(d, tb, ntb)

    def full_rows(b, g, d, tb):
        return rb0 + part_rows(b, g, d, tb)

    if kind == "hgrn":
        qi, z, f_b, lb = inputs
        f_col0 = (heads * dk + heads * dv) // kw
        in_specs = [
            pl.BlockSpec((tbl, kw), lambda b, g, d, tb: (part_rows(b, g, d, tb), g)),
            pl.BlockSpec((tbl, vw), lambda b, g, d, tb: (part_rows(b, g, d, tb), (heads * dk) // vw + g)),
            pl.BlockSpec((tbl, kw), lambda b, g, d, tb: (full_rows(b, g, d, tb), f_col0 + d * ng + g)),
            pl.BlockSpec((1, 1, kw), lambda b, g, d, tb: (d, 0, g)),
            pl.BlockSpec((1, 1, kw), lambda b, g, d, tb: (d, 0, g)),
        ]
        args = [qi, qi, z, f_b, lb]
    else:
        qk, z, a_low, w_a2, b_a = inputs
        rank = a_low.shape[-1]
        in_specs = [
            pl.BlockSpec((tbl, kw), lambda b, g, d, tb: (part_rows(b, g, d, tb), g)),
            pl.BlockSpec((tbl, kw), lambda b, g, d, tb: (part_rows(b, g, d, tb), ng + g)),
            pl.BlockSpec((tbl, vw), lambda b, g, d, tb: (full_rows(b, g, d, tb), (2 * heads * dk) // vw + g)),
            pl.BlockSpec((1, tbl, rank), lambda b, g, d, tb: (d, full_rows(b, g, d, tb), 0)),
            pl.BlockSpec((1, rank, kw), lambda b, g, d, tb: (d, 0, g)),
            pl.BlockSpec((1, 1, kw), lambda b, g, d, tb: (d, 0, g)),
        ]
        args = [qk, qk, z, a_low, w_a2, b_a]
    in_specs.append(mask_spec)
    args.append(masks)
    state_spec = pl.BlockSpec((1, 1, hpb, dk, dv), lambda b, g, d, tb: (b, d, g, 0, 0))
    if init is not None:
        in_specs.append(state_spec)
        args.append(init)
    out_shape = [jax.ShapeDtypeStruct((2, batch * t_len, heads * dv), F32)]
    out_specs = [pl.BlockSpec((1, tbl, vw), lambda b, g, d, tb: (d, part_rows(b, g, d, tb), g))]
    if want_final:
        out_shape.append(jax.ShapeDtypeStruct((batch, 2, heads, dk, dv), F32))
        out_specs.append(state_spec)
    scratch = [pltpu.VMEM((ngrp, nh * dv, nh * dk), F32), pltpu.VMEM((tbl, kw), F32)]
    if kind == "hgrn":
        scratch.append(pltpu.VMEM((tbl, kw), F32))
    res = pl.pallas_call(
        functools.partial(_gated_body, kind, init is not None, want_final, ntb, ngrp, nh, dk, dv, chunk),
        out_shape=out_shape,
        grid=(batch, ng, 2, ntb),
        in_specs=in_specs,
        out_specs=out_specs,
        scratch_shapes=scratch,
        compiler_params=_params("parallel", "parallel", "arbitrary", "arbitrary"),
        name=kind + "_scan",
    )(*args)
    return (res[0], res[1]) if want_final else (res[0], None)


def _head_norm_body(gate_kind, heads, dv, n_ctx_blocks, oc_ref, ol_ref, z_ref, g_ref, y_ref):
    def emit(o_ref):
        for h in range(heads):
            sl = slice(h * dv, (h + 1) * dv)
            o = o_ref[0, :, sl] + o_ref[1, :, sl]
            o = o * lax.rsqrt(jnp.mean(o * o, axis=-1, keepdims=True) + EPS) * g_ref[:, sl]
            zg = z_ref[:, sl]
            gate = _sigmoid(zg) if gate_kind == "sigmoid" else _silu(zg)
            y_ref[:, sl] = (gate * o).astype(y_ref.dtype)

    @pl.when(pl.program_id(0) < n_ctx_blocks)
    def _():
        emit(oc_ref)

    @pl.when(pl.program_id(0) >= n_ctx_blocks)
    def _():
        emit(ol_ref)


def _head_norm_gate(out_ctx, out_lat, z, gate_col0, norm_g, gate_kind):
    _, n_c, hv = out_ctx.shape
    n_l = out_lat.shape[1]
    heads, dv = norm_g.shape
    bm = _pick(math.gcd(n_c, n_l), ROW_BLOCK, 8)
    ncb = n_c // bm
    return pl.pallas_call(
        functools.partial(_head_norm_body, gate_kind, heads, dv, ncb),
        out_shape=jax.ShapeDtypeStruct((n_c + n_l, hv), BF16),
        grid=((n_c + n_l) // bm,),
        in_specs=[pl.BlockSpec((2, bm, hv), lambda i: (0, jnp.minimum(i, ncb - 1), 0)),
                  pl.BlockSpec((2, bm, hv), lambda i: (0, jnp.maximum(i - ncb, 0), 0)),
                  pl.BlockSpec((bm, hv), lambda i: (i, gate_col0 // hv)),
                  pl.BlockSpec((1, hv), lambda i: (0, 0))],
        out_specs=pl.BlockSpec((bm, hv), lambda i: (i, 0)),
        compiler_params=_params("arbitrary"),
        name="head_norm_gate",
    )(out_ctx, out_lat, z, norm_g.reshape(1, hv))


def kernel(x_prompt, x_sample, c, c_ctx, state_mlstm_C, state_mlstm_n, state_mlstm_m, state_hgrn_S,
           state_gla_S, mod_w, mod_b, norm_g, ffn_w_gate, ffn_w_up, ffn_w_down, mlstm_w_in, mlstm_conv,
           mlstm_gate_b, mlstm_norm_g, mlstm_w_out, hgrn_w_in, hgrn_conv, hgrn_f_b, hgrn_lb_logits,
           hgrn_norm_g, hgrn_w_out, gla_w_in, gla_conv, gla_w_a2, gla_b_a, gla_norm_g, gla_w_out,
           final_norm_g):
    bp, tp, d = x_prompt.shape
    bs, ts, _ = x_sample.shape
    n_p, n_s = bp * tp, bs * ts
    depth = mod_w.shape[0]
    a_heads, a_dk, a_dv = state_mlstm_C.shape[3:]
    b_heads, b_dk, b_dv = state_hgrn_S.shape[3:]
    c_heads, c_dk, c_dv = state_gla_S.shape[3:]
    assert 1 + bs <= COND_ROWS and n_p % ts == 0

    x = jnp.concatenate([x_prompt.reshape(n_p, d), x_sample.reshape(n_s, d)], axis=0)
    cond = jnp.zeros((COND_ROWS, d), F32).at[0].set(c_ctx).at[1:1 + bs].set(c)
    mod = _modulation(cond, mod_w, mod_b).reshape(depth, COND_ROWS, N_MOD, d)
    cond_of = _CondRows(n_p, ts)
    parts = ((0, bp, tp, 1, True), (n_p, bs, ts, ts // GRID_W, False))

    lb_p = jax.nn.softmax(hgrn_lb_logits.astype(F32), axis=1)
    lb_all = jnp.cumsum(lb_p, axis=1) - lb_p[:, :1]

    d_ff = ffn_w_gate.shape[-1]
    w_gate = ffn_w_gate.astype(BF16).reshape(-1, d, d_ff)
    w_up = ffn_w_up.astype(BF16).reshape(-1, d, d_ff)
    w_down = ffn_w_down.astype(BF16).reshape(-1, d_ff, d)
    w_outs = (mlstm_w_out.astype(BF16), hgrn_w_out.astype(BF16), gla_w_out.astype(BF16))

    def swiglu(x, l, s, sub):
        h = _norm_mod(x, norm_g[l, sub], mod[l], sub, cond_of)
        a = _swiglu_up(h, w_gate, w_up, 2 * l + s)
        return _matmul_residual(a, w_down, 2 * l + s, x, mod[l], 3 * sub + 2, 0.5, cond_of)

    def mlstm(h, j):
        nq, nv = a_heads * a_dk, a_heads * a_dv
        w_in = mlstm_w_in[j]
        z = _matmul(h, w_in[:, :2 * nq + 2 * nv].astype(BF16), F32)
        n_tail = w_in.shape[1] - (2 * nq + 2 * nv)
        w_tail = jnp.pad(w_in[:, 2 * nq + 2 * nv:], ((0, 0), (0, 128 - n_tail))).astype(BF16)
        zt = _matmul(h, w_tail, F32, bn_pref=128)[:, :n_tail]
        gates = zt.reshape(-1, 2, 2, a_heads).transpose(1, 3, 0, 2)
        gate_b = mlstm_gate_b[j].astype(F32).transpose(0, 2, 1).reshape(2, a_heads, 1, 2)
        outs, finals = [], None
        for row0, batch, t_len, rows, is_ctx in parts:
            qk = _conv_silu(z, mlstm_conv[j], 2 * nq, row0, batch, t_len, rows)
            init = None if is_ctx else (state_mlstm_C[:, j], state_mlstm_n[:, j], state_mlstm_m[:, j])
            out2, fin = _mlstm_scan(qk, z, gates, gate_b, init, row0, batch, t_len, a_heads, a_dk, a_dv,
                                    max(g for g in (1, 2, 4) if a_heads % g == 0), is_ctx)
            outs.append(out2)
            finals = fin if is_ctx else finals
        return _head_norm_gate(outs[0], outs[1], z, 2 * nq + nv, mlstm_norm_g[j], "sigmoid"), finals

    def hgrn(h, j, l):
        nq, ni = b_heads * b_dk, b_heads * b_dv
        z = _matmul(h, hgrn_w_in[j].astype(BF16), F32)
        f_b = hgrn_f_b[j].astype(F32).reshape(2, 1, nq)
        lb = lb_all[:, l].reshape(2, 1, nq)
        nh = 2 if b_heads % 2 == 0 else 1
        ngrp = max(g for g in (1, 2, 4) if b_heads % (g * nh) == 0)
        outs, finals = [], None
        for row0, batch, t_len, rows, is_ctx in parts:
            qi = _conv_silu(z, hgrn_conv[j], nq + ni, row0, batch, t_len, rows)
            init = None if is_ctx else state_hgrn_S[:, j]
            out2, fin = _gated_scan("hgrn", (qi, z, f_b, lb), init, row0, batch, t_len,
                                    b_heads, b_dk, b_dv, ngrp, nh, is_ctx)
            outs.append(out2)
            finals = (fin,) if is_ctx else finals
        return _head_norm_gate(outs[0], outs[1], z, 3 * nq + ni, hgrn_norm_g[j], "silu"), finals

    def gla(h, j):
        nq, nv = c_heads * c_dk, c_heads * c_dv
        w_in = gla_w_in[j]
        z = _matmul(h, w_in[:, :2 * nq + 2 * nv].astype(BF16), F32)
        n_tail = w_in.shape[1] - (2 * nq + 2 * nv)
        w_tail = jnp.pad(w_in[:, 2 * nq + 2 * nv:], ((0, 0), (0, 128 - n_tail))).astype(BF16)
        zt = _matmul(h, w_tail, F32, bn_pref=128)[:, :n_tail]
        a_low = zt.reshape(-1, 2, n_tail // 2).transpose(1, 0, 2)
        b_a = gla_b_a[j].astype(F32).reshape(2, 1, nq)
        outs, finals = [], None
        for row0, batch, t_len, rows, is_ctx in parts:
            qk = _conv_silu(z, gla_conv[j], 2 * nq, row0, batch, t_len, rows)
            init = None if is_ctx else state_gla_S[:, j]
            out2, fin = _gated_scan("gla", (qk, z, a_low, gla_w_a2[j], b_a), init, row0, batch, t_len,
                                    c_heads, c_dk, c_dv, 1, 1, is_ctx)
            outs.append(out2)
            finals = (fin,) if is_ctx else finals
        return _head_norm_gate(outs[0], outs[1], z, 2 * nq + nv, gla_norm_g[j], "silu"), finals

    states = []
    for l in range(depth):
        x = swiglu(x, l, 0, 0)
        h = _norm_mod(x, norm_g[l, 1], mod[l], 1, cond_of)
        j, kind = l // 3, l % 3
        y, fin = mlstm(h, j) if kind == 0 else hgrn(h, j, l) if kind == 1 else gla(h, j)
        x = _matmul_residual(y, w_outs[kind], j, x, mod[l], 5, 1.0, cond_of)
        x = swiglu(x, l, 1, 2)
        states.append(fin)

    y = _final_norm(x, final_norm_g)
    y_prompt = y[:n_p].reshape(bp, tp, d)
    y_sample = y[n_p:].reshape(bs, ts, d)
    new_c = jnp.stack([states[l][0] for l in range(0, depth, 3)], axis=1)
    new_n = jnp.stack([states[l][1] for l in range(0, depth, 3)], axis=1)
    new_m = jnp.stack([states[l][2] for l in range(0, depth, 3)], axis=1)
    new_hgrn = jnp.stack([states[l][0] for l in range(1, depth, 3)], axis=1)
    new_gla = jnp.stack([states[l][0] for l in range(2, depth, 3)], axis=1)
    return (y_prompt, y_sample, new_c, new_n, new_m, new_hgrn, new_gla)
```

```python
import functools
import math

import jax
import jax.numpy as jnp
import numpy as np
from jax import lax
from jax.experimental import pallas as pl
from jax.experimental.pallas import tpu as pltpu

F32 = jnp.float32
BF16 = jnp.bfloat16

GRID_W = 64
EPS = 1e-6
GLA_TAU = 16.0
MLSTM_CHUNK = 64
GATED_CHUNK = {"hgrn": 128, "gla": 256}
N_MOD = 9
COND_ROWS = 8

VMEM_LIMIT = 56 * 1024 * 1024
SCAN_TIME_BLOCK = 512
ROW_BLOCK = 256
ROW_TILE = 16


def _params(*sem):
    return pltpu.CompilerParams(dimension_semantics=sem, vmem_limit_bytes=VMEM_LIMIT)


def _pick(n, pref, align):
    if n <= pref:
        return n
    best = None
    for cand in range(align, pref + 1, align):
        if n % cand == 0:
            best = cand
    assert best is not None, (n, pref, align)
    return best


def _sigmoid(x):
    return 1.0 / (1.0 + jnp.exp(-x))


def _silu(x):
    return x * _sigmoid(x)


def _log_sigmoid(x):
    return jnp.minimum(x, 0.0) - jnp.log1p(jnp.exp(-jnp.abs(x)))


class _CondRows:
    def __init__(self, n_prompt_rows, t_sample):
        assert n_prompt_rows % t_sample == 0
        self.n_prompt_rows, self.t_sample = n_prompt_rows, t_sample

    def block_rows(self, pref):
        return _pick(self.t_sample, pref, 8)

    def index(self, bm):
        assert self.t_sample % bm == 0

        def f(i):
            start = i * bm
            return jnp.where(start < self.n_prompt_rows, 0, 1 + (start - self.n_prompt_rows) // self.t_sample)
        return f


def _mod_body(c_ref, w_ref, b_ref, o_ref):
    a = _silu(c_ref[...]).astype(BF16)
    o_ref[0] = jnp.dot(a, w_ref[0].astype(BF16), preferred_element_type=F32) + b_ref[0]


def _modulation(cond, mod_w, mod_b):
    depth, d, n = mod_w.shape
    bn = _pick(n, 512, 128)
    return pl.pallas_call(
        _mod_body,
        out_shape=jax.ShapeDtypeStruct((depth, COND_ROWS, n), F32),
        grid=(depth, n // bn),
        in_specs=[pl.BlockSpec((COND_ROWS, d), lambda l, j: (0, 0)),
                  pl.BlockSpec((1, d, bn), lambda l, j: (l, 0, j)),
                  pl.BlockSpec((1, 1, bn), lambda l, j: (l, 0, j))],
        out_specs=pl.BlockSpec((1, COND_ROWS, bn), lambda l, j: (l, 0, j)),
        compiler_params=_params("arbitrary", "arbitrary"),
        name="modulation",
    )(cond, mod_w, mod_b.reshape(depth, 1, n))


def _row_tiles(n_rows, tile, fn):
    tile = min(tile, n_rows)

    def step(r, carry):
        fn(pl.ds(pl.multiple_of(r * tile, tile), tile))
        return carry
    lax.fori_loop(0, n_rows // tile, step, 0, unroll=4)


def _norm_mod_body(sub, x_ref, g_ref, m_ref, o_ref):
    m = m_ref[0]
    gain = g_ref[...] * (1.0 + m[3 * sub + 1:3 * sub + 2, :])
    shift = m[3 * sub:3 * sub + 1, :]

    def tile(rows):
        x = x_ref[rows, :]
        y = x * lax.rsqrt(jnp.mean(x * x, axis=-1, keepdims=True) + EPS)
        o_ref[rows, :] = (y * gain + shift).astype(o_ref.dtype)
    _row_tiles(x_ref.shape[0], ROW_TILE, tile)


def _norm_mod(x, g, mod_l, sub, cond_of):
    n, d = x.shape
    bm = cond_of.block_rows(ROW_BLOCK)
    cr = cond_of.index(bm)
    return pl.pallas_call(
        functools.partial(_norm_mod_body, sub),
        out_shape=jax.ShapeDtypeStruct((n, d), BF16),
        grid=(n // bm,),
        in_specs=[pl.BlockSpec((bm, d), lambda i: (i, 0)),
                  pl.BlockSpec((1, d), lambda i: (0, 0)),
                  pl.BlockSpec((1, N_MOD, d), lambda i: (cr(i), 0, 0))],
        out_specs=pl.BlockSpec((bm, d), lambda i: (i, 0)),
        compiler_params=_params("parallel"),
        name="norm_mod",
    )(x, g.reshape(1, d), mod_l)


def _final_norm_body(x_ref, g_ref, o_ref):
    gain = g_ref[...]

    def tile(rows):
        x = x_ref[rows, :]
        o_ref[rows, :] = x * lax.rsqrt(jnp.mean(x * x, axis=-1, keepdims=True) + EPS) * gain
    _row_tiles(x_ref.shape[0], ROW_TILE, tile)


def _final_norm(x, g):
    n, d = x.shape
    bm = _pick(n, ROW_BLOCK, 8)
    return pl.pallas_call(
        _final_norm_body,
        out_shape=jax.ShapeDtypeStruct((n, d), F32),
        grid=(n // bm,),
        in_specs=[pl.BlockSpec((bm, d), lambda i: (i, 0)), pl.BlockSpec((1, d), lambda i: (0, 0))],
        out_specs=pl.BlockSpec((bm, d), lambda i: (i, 0)),
        compiler_params=_params("parallel"),
        name="final_norm",
    )(x, g.reshape(1, d))


def _matmul_body(x_ref, w_ref, o_ref):
    o_ref[...] = jnp.dot(x_ref[...], w_ref[...], preferred_element_type=F32).astype(o_ref.dtype)


def _matmul(x, w, out_dtype, bm_pref=1024, bn_pref=512):
    n, k = x.shape
    p = w.shape[1]
    bm = _pick(n, bm_pref, 8)
    bn = _pick(p, bn_pref, 128)
    return pl.pallas_call(
        _matmul_body,
        out_shape=jax.ShapeDtypeStruct((n, p), out_dtype),
        grid=(n // bm, p // bn),
        in_specs=[pl.BlockSpec((bm, k), lambda i, j: (i, 0)),
                  pl.BlockSpec((k, bn), lambda i, j: (0, j))],
        out_specs=pl.BlockSpec((bm, bn), lambda i, j: (i, j)),
        compiler_params=_params("parallel", "arbitrary"),
        name="matmul",
    )(x, w)


def _swiglu_up_body(h_ref, wg_ref, wu_ref, o_ref):
    h = h_ref[...]
    g = jnp.dot(h, wg_ref[...], preferred_element_type=F32)
    u = jnp.dot(h, wu_ref[...], preferred_element_type=F32)
    o_ref[...] = (_silu(g) * u).astype(o_ref.dtype)


def _swiglu_up(h, wg, wu, wi, bm_pref=1024, bn_pref=256):
    n, k = h.shape
    f = wg.shape[2]
    bm = _pick(n, bm_pref, 8)
    bn = _pick(f, bn_pref, 128)
    return pl.pallas_call(
        _swiglu_up_body,
        out_shape=jax.ShapeDtypeStruct((n, f), BF16),
        grid=(n // bm, f // bn),
        in_specs=[pl.BlockSpec((bm, k), lambda i, j: (i, 0)),
                  pl.BlockSpec((pl.Squeezed(), k, bn), lambda i, j: (wi, 0, j)),
                  pl.BlockSpec((pl.Squeezed(), k, bn), lambda i, j: (wi, 0, j))],
        out_specs=pl.BlockSpec((bm, bn), lambda i, j: (i, j)),
        compiler_params=_params("parallel", "arbitrary"),
        name="swiglu_up",
    )(h, wg, wu)


def _matmul_residual_body(gate_row, coef, a_ref, w_ref, x_ref, m_ref, o_ref):
    y = jnp.dot(a_ref[...], w_ref[...], preferred_element_type=F32)
    gate = m_ref[0][gate_row:gate_row + 1, :]
    o_ref[...] = x_ref[...] + (coef * gate) * y


def _matmul_residual(a, w, wi, x, mod_l, gate_row, coef, cond_of, bm_pref=512, bn_pref=512):
    n, k = a.shape
    d = w.shape[2]
    bm = cond_of.block_rows(bm_pref)
    bn = _pick(d, bn_pref, 128)
    cr = cond_of.index(bm)
    return pl.pallas_call(
        functools.partial(_matmul_residual_body, gate_row, coef),
        out_shape=jax.ShapeDtypeStruct((n, d), F32),
        grid=(n // bm, d // bn),
        in_specs=[pl.BlockSpec((bm, k), lambda i, j: (i, 0)),
                  pl.BlockSpec((pl.Squeezed(), k, bn), lambda i, j: (wi, 0, j)),
                  pl.BlockSpec((bm, bn), lambda i, j: (i, j)),
                  pl.BlockSpec((1, N_MOD, bn), lambda i, j: (cr(i), 0, j))],
        out_specs=pl.BlockSpec((bm, bn), lambda i, j: (i, j)),
        compiler_params=_params("parallel", "arbitrary"),
        name="matmul_residual",
    )(a, w, x, mod_l)


def _conv_body(rows, cols, z_ref, w_ref, o_ref):
    u = z_ref[...]
    t_len = u.shape[0]
    t = lax.broadcasted_iota(jnp.int32, (t_len, 1), 0)
    if rows == 1:
        r, c = jnp.zeros_like(t), t
    else:
        r, c = t >> (cols.bit_length() - 1), t & (cols - 1)
    w = w_ref[...]
    taps = (jnp.where(c >= 1, pltpu.roll(u, 1, 0), 0.0), u, jnp.where(c < cols - 1, pltpu.roll(u, t_len - 1, 0), 0.0))
    acc = None
    for i in range(3):
        if rows == 1 and i != 1:
            continue
        part = sum(taps[j] * w[3 * i + j:3 * i + j + 1, :] for j in range(3))
        if i != 1:
            part = jnp.where((r + (i - 1) >= 0) & (r + (i - 1) < rows),
                             pltpu.roll(part, (-(i - 1) * cols) % t_len, 0), 0.0)
        acc = part if acc is None else acc + part
    o_ref[...] = _silu(acc)


def _conv_silu(z, conv_w, n_conv, row0, batch, t_len, rows):
    cols = t_len // rows
    assert rows == 1 or (cols & (cols - 1)) == 0
    cb = _pick(n_conv, 256, 128)
    rb0 = row0 // t_len
    return pl.pallas_call(
        functools.partial(_conv_body, rows, cols),
        out_shape=jax.ShapeDtypeStruct((batch * t_len, n_conv), F32),
        grid=(batch, n_conv // cb),
        in_specs=[pl.BlockSpec((t_len, cb), lambda b, j: (rb0 + b, j)),
                  pl.BlockSpec((9, cb), lambda b, j: (0, j))],
        out_specs=pl.BlockSpec((t_len, cb), lambda b, j: (b, j)),
        compiler_params=_params("parallel", "arbitrary"),
        name="conv_silu",
    )(z, conv_w.reshape(9, n_conv))


def _time_block(d, tb, ntb):
    return tb + d * (ntb - 1 - 2 * tb)


def _dir_sign(d):
    return 1 - 2 * d


def _mlstm_body(has_init, want_final, ntb, hpb, dk, dv, *refs):
    it = iter(refs)
    q_ref, k_ref, v_ref, g_ref, gb_ref = (next(it) for _ in range(5))
    if has_init:
        c0_ref, n0_ref, m0_ref = (next(it) for _ in range(3))
    o_ref = next(it)
    if want_final:
        cf_ref, nf_ref, mf_ref = (next(it) for _ in range(3))
    c_s, n_s, m_s = (next(it) for _ in range(3))

    d = pl.program_id(2)
    tb = pl.program_id(3)
    L = MLSTM_CHUNK
    tbl = q_ref.shape[0]
    q_scale = dk ** -0.5

    @pl.when(tb == 0)
    def _():
        if has_init:
            c_s[...] = c0_ref[0, 0]
            n_s[...] = n0_ref[0, 0]
            m_s[...] = m0_ref[0, 0]
        else:
            c_s[...] = jnp.zeros_like(c_s)
            n_s[...] = jnp.zeros_like(n_s)
            m_s[...] = jnp.zeros_like(m_s)

    sign = _dir_sign(d)
    sub = lax.broadcasted_iota(jnp.int32, (L, L), 0)
    lane = lax.broadcasted_iota(jnp.int32, (L, L), 1)
    sees = sign * (sub - lane) >= 0
    sees_t = sign * (lane - sub) >= 0
    eye = sub == lane
    ncl = tbl // L

    def chunk(ci, carry):
        lc = ci + d * (ncl - 1 - 2 * ci)
        r0 = pl.multiple_of(lc * L, L)
        rows = pl.ds(r0, L)
        loaded = []
        for hh in range(hpb):
            ks = slice(hh * dk, (hh + 1) * dk)
            vs = slice(hh * dv, (hh + 1) * dv)
            loaded.append((q_ref[rows, ks], k_ref[rows, ks], v_ref[rows, vs], g_ref[0, hh, rows, :] + gb_ref[0, hh],
                           c_s[hh], n_s[hh], m_s[hh]))
        hs = range(hpb)
        q_raw, kc, v_raw, gates, c_st, n_st, m_st = zip(*loaded)
        qc = [(q_raw[h] * q_scale).astype(BF16) for h in hs]
        vc = [v_raw[h].astype(BF16) for h in hs]
        i_col = [gates[h][:, 0:1] for h in hs]
        f_col = [_log_sigmoid(gates[h][:, 1:2]) for h in hs]
        qk = [_dot_nt(qc[h], kc[h].astype(BF16)) for h in hs]
        qc_c = [jnp.dot(qc[h], c_st[h].astype(BF16), preferred_element_type=F32) for h in hs]
        b_row = [jnp.sum(jnp.where(sees_t, f_col[h], 0.0), axis=0, keepdims=True) for h in hs]
        i_row = [jnp.sum(jnp.where(eye, i_col[h], 0.0), axis=0, keepdims=True) for h in hs]
        b_end = [jnp.sum(f_col[h], axis=0, keepdims=True) for h in hs]
        b_col = [jnp.sum(jnp.where(eye, b_row[h], 0.0), axis=1, keepdims=True) for h in hs]
        dmat = [jnp.where(sees, b_col[h] - b_row[h] + i_row[h], -jnp.inf) for h in hs]
        w = [b_end[h] - b_col[h] + i_col[h] for h in hs]
        d_max = [jnp.max(dmat[h], axis=1, keepdims=True) for h in hs]
        w_max = [jnp.max(w[h], axis=0, keepdims=True) for h in hs]
        m_inter = [b_col[h] + m_st[h] for h in hs]
        m_t = [jnp.maximum(m_inter[h], d_max[h]) for h in hs]
        m_new = [jnp.maximum(b_end[h] + m_st[h], w_max[h]) for h in hs]
        s = [qk[h] * jnp.exp(dmat[h] - m_t[h]) for h in hs]
        e_inter = [jnp.exp(m_inter[h] - m_t[h]) for h in hs]
        kw = [kc[h] * jnp.exp(w[h] - m_new[h]) for h in hs]
        decay = [jnp.exp(b_end[h] + m_st[h] - m_new[h]) for h in hs]
        num = [jnp.dot(s[h].astype(BF16), vc[h], preferred_element_type=F32) + e_inter[h] * qc_c[h] for h in hs]
        kv = [_dot_tn(kw[h].astype(BF16), vc[h]) for h in hs]
        qn = [jnp.sum(qc[h].astype(F32) * n_st[h].astype(BF16).astype(F32), axis=1, keepdims=True) for h in hs]
        den = [jnp.sum(s[h], axis=1, keepdims=True) + e_inter[h] * qn[h] for h in hs]
        results = [(num[h] / jnp.maximum(jnp.abs(den[h]), jnp.exp(-m_t[h])), decay[h] * c_st[h] + kv[h],
                    decay[h] * n_st[h] + jnp.sum(kw[h], axis=0, keepdims=True), m_new[h]) for h in hs]
        for hh, (out, c_new, n_new, m_new) in enumerate(results):
            o_ref[0, rows, hh * dv:(hh + 1) * dv] = out
            c_s[hh] = c_new
            n_s[hh] = n_new
            m_s[hh] = m_new
        return carry

    lax.fori_loop(0, ncl, chunk, 0)

    if want_final:
        @pl.when(tb == ntb - 1)
        def _():
            cf_ref[0, 0] = c_s[...]
            nf_ref[0, 0] = n_s[...]
            mf_ref[0, 0] = m_s[...]


def _mlstm_scan(qk, z, gates, gate_b, init, row0, batch, t_len, heads, dk, dv, hpb, want_final):
    tbl = _pick(t_len, SCAN_TIME_BLOCK, MLSTM_CHUNK)
    ntb = t_len // tbl
    rb0 = row0 // tbl
    ng = heads // hpb
    v_col0 = (2 * heads * dk) // (hpb * dv)

    def part_rows(b, h, d, tb):
        return b * ntb + _time_block(d, tb, ntb)

    in_specs = [
        pl.BlockSpec((tbl, hpb * dk), lambda b, h, d, tb: (part_rows(b, h, d, tb), h)),
        pl.BlockSpec((tbl, hpb * dk), lambda b, h, d, tb: (part_rows(b, h, d, tb), ng + h)),
        pl.BlockSpec((tbl, hpb * dv), lambda b, h, d, tb: (rb0 + part_rows(b, h, d, tb), v_col0 + h)),
        pl.BlockSpec((1, hpb, tbl, 2), lambda b, h, d, tb: (d, h, rb0 + part_rows(b, h, d, tb), 0)),
        pl.BlockSpec((1, hpb, 1, 2), lambda b, h, d, tb: (d, h, 0, 0)),
    ]
    args = [qk, qk, z, gates, gate_b]
    state_specs = [
        pl.BlockSpec((1, 1, hpb, dk, dv), lambda b, h, d, tb: (b, d, h, 0, 0)),
        pl.BlockSpec((1, 1, hpb, 1, dk), lambda b, h, d, tb: (b, d, h, 0, 0)),
        pl.BlockSpec((1, 1, hpb, 1, 1), lambda b, h, d, tb: (b, d, h, 0, 0)),
    ]
    if init is not None:
        c0, n0, m0 = init
        in_specs += state_specs
        args += [c0, n0.reshape(batch, 2, heads, 1, dk), m0.reshape(batch, 2, heads, 1, 1)]
    out_shape = [jax.ShapeDtypeStruct((2, batch * t_len, heads * dv), F32)]
    out_specs = [pl.BlockSpec((1, tbl, hpb * dv), lambda b, h, d, tb: (d, part_rows(b, h, d, tb), h))]
    if want_final:
        out_shape += [jax.ShapeDtypeStruct((batch, 2, heads, dk, dv), F32),
                      jax.ShapeDtypeStruct((batch, 2, heads, 1, dk), F32),
                      jax.ShapeDtypeStruct((batch, 2, heads, 1, 1), F32)]
        out_specs += state_specs
    res = pl.pallas_call(
        functools.partial(_mlstm_body, init is not None, want_final, ntb, hpb, dk, dv),
        out_shape=out_shape,
        grid=(batch, ng, 2, ntb),
        in_specs=in_specs,
        out_specs=out_specs,
        scratch_shapes=[pltpu.VMEM((hpb, dk, dv), F32), pltpu.VMEM((hpb, 1, dk), F32),
                        pltpu.VMEM((hpb, 1, 1), F32)],
        compiler_params=_params("parallel", "parallel", "arbitrary", "arbitrary"),
        name="mlstm_scan",
    )(*args)
    if want_final:
        out, cf, nf, mf = res
        return out, (cf, nf.reshape(batch, 2, heads, dk), mf.reshape(batch, 2, heads))
    return res[0], None


def _pair_masks(L, nh):
    p = L.bit_length() - 1
    t = np.arange(L)[:, None]
    s = np.arange(L)[None, :]
    out = np.zeros((2, p + 1, L, L), np.float32)
    for j in range(p):
        same = (t >> (j + 1)) == (s >> (j + 1))
        t_hi, s_hi = ((t >> j) & 1) == 1, ((s >> j) & 1) == 1
        out[0, j] = same & t_hi & ~s_hi
        out[1, j] = same & ~t_hi & s_hi
    out[:, p] = np.eye(L, dtype=np.float32)
    return jnp.asarray(np.tile(out, (1, 1, 1, nh)).reshape(2, (p + 1) * L, nh * L))


def _block_rows(x, blk, r):
    n, w = x.shape
    if blk >= 16:
        return jnp.concatenate([jnp.broadcast_to(x[b0 + r:b0 + r + 1, :], (blk, w)) for b0 in range(0, n, blk)],
                               axis=0)
    x3 = x.reshape(n // 8, 8, w)
    sub = lax.broadcasted_iota(jnp.int32, (1, 8, 1), 1)
    out = None
    for s0 in range(0, 8, blk):
        bc = jnp.broadcast_to(x3[:, s0 + r:s0 + r + 1, :], x3.shape)
        out = bc if out is None else jnp.where(sub >= s0, bc, out)
    return out.reshape(n, w)


def _head_diag_rows(x, nh, w):
    if nh == 1:
        return x
    lane_head = lax.broadcasted_iota(jnp.int32, (1, nh * w), 1) // w
    return jnp.concatenate([jnp.where(lane_head == h, x, jnp.zeros_like(x)) for h in range(nh)], axis=0)


def _dot_nt(a, b):
    return lax.dot_general(a, b, (((1,), (1,)), ((), ())), preferred_element_type=F32)


def _dot_tn(a, b):
    return lax.dot_general(a, b, (((0,), (0,)), ((), ())), preferred_element_type=F32)


def _gated_body(kind, has_init, want_final, ntb, ngrp, nh, dk, dv, L, *refs):
    it = iter(refs)
    if kind == "hgrn":
        q_ref, v_ref, fpre_ref, fb_ref, lb_ref = (next(it) for _ in range(5))
    else:
        q_ref, k_ref, v_ref, alow_ref, wa_ref, ba_ref = (next(it) for _ in range(6))
    m_ref = next(it)
    if has_init:
        s0_ref = next(it)
    o_ref = next(it)
    if want_final:
        sf_ref = next(it)
    st_s, lg_s = next(it), next(it)
    if kind == "hgrn":
        kk_s = next(it)

    d = pl.program_id(2)
    tb = pl.program_id(3)
    tbl = q_ref.shape[0]
    p = L.bit_length() - 1
    kw, vw = nh * dk, nh * dv
    q_scale = 1.0 if kind == "hgrn" else dk ** -0.5

    @pl.when(tb == 0)
    def _():
        st_s[...] = jnp.zeros_like(st_s)
        if has_init:
            for gi in range(ngrp):
                for h in range(nh):
                    st_s[gi, h * dv:(h + 1) * dv, h * dk:(h + 1) * dk] = s0_ref[0, 0, gi * nh + h].T

    if kind == "hgrn":
        lb = lb_ref[0]
        f = lb + (1.0 - lb) * _sigmoid(fpre_ref[...] + fb_ref[0])
        lg_s[...] = jnp.log(f)
        kk_s[...] = 1.0 - f
    else:
        pre = jnp.dot(alow_ref[0].astype(BF16), wa_ref[0].astype(BF16), preferred_element_type=F32)
        lg_s[...] = _log_sigmoid(pre + ba_ref[0]) / GLA_TAU

    row = lax.broadcasted_iota(jnp.int32, (L, 1), 0)
    if nh > 1:
        own = ((lax.broadcasted_iota(jnp.int32, (vw, kw), 0) // dv)
               == (lax.broadcasted_iota(jnp.int32, (vw, kw), 1) // dk))
    ncl = tbl // L

    def run(rev):
        def chunk(ci, carry):
            r0 = pl.multiple_of(((ncl - 1 - ci) if rev else ci) * L, L)
            rows = pl.ds(r0, L)
            loaded = []
            for gi in range(ngrp):
                ks = slice(gi * kw, (gi + 1) * kw)
                vs = slice(gi * vw, (gi + 1) * vw)
                loaded.append((q_ref[rows, ks], kk_s[rows, ks] if kind == "hgrn" else k_ref[rows, ks],
                               v_ref[rows, vs], lg_s[rows, ks], st_s[gi]))
            results = []
            for q, k, v, seg, st in loaded:
                q = q * q_scale
                v = v.astype(BF16)
                a = m_ref[0, p * L:(p + 1) * L, :] * _dot_nt(q.astype(BF16), _head_diag_rows(k.astype(BF16), nh, dk))
                for j in range(p):
                    half = 1 << j
                    edge = _block_rows(seg, 2 * half, half if rev else half - 1)
                    later = (((row >> j) & 1) == 1) != rev
                    z = (jnp.exp(jnp.where(later, seg, edge - seg)) * jnp.where(later, q, k)).astype(BF16)
                    a = a + m_ref[0, j * L:(j + 1) * L, :] * _dot_nt(z, _head_diag_rows(z, nh, dk))
                    seg = seg + jnp.where(later, edge, 0.0)
                total = seg[0:1, :] if rev else seg[L - 1:L, :]
                out = (jnp.dot(a.astype(BF16), _head_diag_rows(v, nh, dv), preferred_element_type=F32)
                       + _dot_nt((q * jnp.exp(seg)).astype(BF16), st.astype(BF16)))
                st = st * jnp.exp(total) + _dot_tn(v, (k * jnp.exp(total - seg)).astype(BF16))
                results.append((out, st if nh == 1 else jnp.where(own, st, 0.0)))
            for gi, (out, st) in enumerate(results):
                o_ref[0, rows, gi * vw:(gi + 1) * vw] = out
                st_s[gi] = st
            return carry

        lax.fori_loop(0, ncl, chunk, 0)

    @pl.when(d == 0)
    def _():
        run(False)

    @pl.when(d == 1)
    def _():
        run(True)

    if want_final:
        @pl.when(tb == ntb - 1)
        def _():
            for gi in range(ngrp):
                for h in range(nh):
                    sf_ref[0, 0, gi * nh + h] = st_s[gi, h * dv:(h + 1) * dv, h * dk:(h + 1) * dk].T


def _gated_scan(kind, inputs, init, row0, batch, t_len, heads, dk, dv, ngrp, nh, want_final):
    chunk = min(GATED_CHUNK[kind], SCAN_TIME_BLOCK, t_len)
    tbl = _pick(t_len, SCAN_TIME_BLOCK, chunk)
    ntb = t_len // tbl
    rb0 = row0 // tbl
    hpb = ngrp * nh
    ng = heads // hpb
    kw, vw = hpb * dk, hpb * dv
    masks = _pair_masks(chunk, nh)
    mask_spec = pl.BlockSpec((1,) + masks.shape[1:], lambda b, g, d, tb: (d, 0, 0))

    def part_rows(b, g, d, tb):
        return b * ntb + _time_block(d, tb, ntb)

    def full_rows(b, g, d, tb):
        return rb0 + part_rows(b, g, d, tb)

    if kind == "hgrn":
        qi, z, f_b, lb = inputs
        f_col0 = (heads * dk + heads * dv) // kw
        in_specs = [
            pl.BlockSpec((tbl, kw), lambda b, g, d, tb: (part_rows(b, g, d, tb), g)),
            pl.BlockSpec((tbl, vw), lambda b, g, d, tb: (part_rows(b, g, d, tb), (heads * dk) // vw + g)),
            pl.BlockSpec((tbl, kw), lambda b, g, d, tb: (full_rows(b, g, d, tb), f_col0 + d * ng + g)),
            pl.BlockSpec((1, 1, kw), lambda b, g, d, tb: (d, 0, g)),
            pl.BlockSpec((1, 1, kw), lambda b, g, d, tb: (d, 0, g)),
        ]
        args = [qi, qi, z, f_b, lb]
    else:
        qk, z, a_low, w_a2, b_a = inputs
        rank = a_low.shape[-1]
        in_specs = [
            pl.BlockSpec((tbl, kw), lambda b, g, d, tb: (part_rows(b, g, d, tb), g)),
            pl.BlockSpec((tbl, kw), lambda b, g, d, tb: (part_rows(b, g, d, tb), ng + g)),
            pl.BlockSpec((tbl, vw), lambda b, g, d, tb: (full_rows(b, g, d, tb), (2 * heads * dk) // vw + g)),
            pl.BlockSpec((1, tbl, rank), lambda b, g, d, tb: (d, full_rows(b, g, d, tb), 0)),
            pl.BlockSpec((1, rank, kw), lambda b, g, d, tb: (d, 0, g)),
            pl.BlockSpec((1, 1, kw), lambda b, g, d, tb: (d, 0, g)),
        ]
        args = [qk, qk, z, a_low, w_a2, b_a]
    in_specs.append(mask_spec)
    args.append(masks)
    state_spec = pl.BlockSpec((1, 1, hpb, dk, dv), lambda b, g, d, tb: (b, d, g, 0, 0))
    if init is not None:
        in_specs.append(state_spec)
        args.append(init)
    out_shape = [jax.ShapeDtypeStruct((2, batch * t_len, heads * dv), F32)]
    out_specs = [pl.BlockSpec((1, tbl, vw), lambda b, g, d, tb: (d, part_rows(b, g, d, tb), g))]
    if want_final:
        out_shape.append(jax.ShapeDtypeStruct((batch, 2, heads, dk, dv), F32))
        out_specs.append(state_spec)
    scratch = [pltpu.VMEM((ngrp, nh * dv, nh * dk), F32), pltpu.VMEM((tbl, kw), F32)]
    if kind == "hgrn":
        scratch.append(pltpu.VMEM((tbl, kw), F32))
    res = pl.pallas_call(
        functools.partial(_gated_body, kind, init is not None, want_final, ntb, ngrp, nh, dk, dv, chunk),
        out_shape=out_shape,
        grid=(batch, ng, 2, ntb),
        in_specs=in_specs,
        out_specs=out_specs,
        scratch_shapes=scratch,
        compiler_params=_params("parallel", "parallel", "arbitrary", "arbitrary"),
        name=kind + "_scan",
    )(*args)
    return (res[0], res[1]) if want_final else (res[0], None)


def _head_norm_body(gate_kind, heads, dv, n_ctx_blocks, oc_ref, ol_ref, z_ref, g_ref, y_ref):
    def emit(o_ref):
        def tile(rows):
            for h in range(heads):
                sl = slice(h * dv, (h + 1) * dv)
                o = o_ref[0, rows, sl] + o_ref[1, rows, sl]
                o = o * lax.rsqrt(jnp.mean(o * o, axis=-1, keepdims=True) + EPS) * g_ref[:, sl]
                zg = z_ref[rows, sl]
                gate = _sigmoid(zg) if gate_kind == "sigmoid" else _silu(zg)
                y_ref[rows, sl] = (gate * o).astype(y_ref.dtype)
        _row_tiles(y_ref.shape[0], ROW_TILE, tile)

    @pl.when(pl.program_id(0) < n_ctx_blocks)
    def _():
        emit(oc_ref)

    @pl.when(pl.program_id(0) >= n_ctx_blocks)
    def _():
        emit(ol_ref)


def _head_norm_gate(out_ctx, out_lat, z, gate_col0, norm_g, gate_kind):
    _, n_c, hv = out_ctx.shape
    n_l = out_lat.shape[1]
    heads, dv = norm_g.shape
    bm = _pick(math.gcd(n_c, n_l), ROW_BLOCK, 8)
    ncb = n_c // bm
    return pl.pallas_call(
        functools.partial(_head_norm_body, gate_kind, heads, dv, ncb),
        out_shape=jax.ShapeDtypeStruct((n_c + n_l, hv), BF16),
        grid=((n_c + n_l) // bm,),
        in_specs=[pl.BlockSpec((2, bm, hv), lambda i: (0, jnp.minimum(i, ncb - 1), 0)),
                  pl.BlockSpec((2, bm, hv), lambda i: (0, jnp.maximum(i - ncb, 0), 0)),
                  pl.BlockSpec((bm, hv), lambda i: (i, gate_col0 // hv)),
                  pl.BlockSpec((1, hv), lambda i: (0, 0))],
        out_specs=pl.BlockSpec((bm, hv), lambda i: (i, 0)),
        compiler_params=_params("arbitrary"),
        name="head_norm_gate",
    )(out_ctx, out_lat, z, norm_g.reshape(1, hv))


def kernel(x_prompt, x_sample, c, c_ctx, state_mlstm_C, state_mlstm_n, state_mlstm_m, state_hgrn_S,
           state_gla_S, mod_w, mod_b, norm_g, ffn_w_gate, ffn_w_up, ffn_w_down, mlstm_w_in, mlstm_conv,
           mlstm_gate_b, mlstm_norm_g, mlstm_w_out, hgrn_w_in, hgrn_conv, hgrn_f_b, hgrn_lb_logits,
           hgrn_norm_g, hgrn_w_out, gla_w_in, gla_conv, gla_w_a2, gla_b_a, gla_norm_g, gla_w_out,
           final_norm_g):
    bp, tp, d = x_prompt.shape
    bs, ts, _ = x_sample.shape
    n_p, n_s = bp * tp, bs * ts
    depth = mod_w.shape[0]
    a_heads, a_dk, a_dv = state_mlstm_C.shape[3:]
    b_heads, b_dk, b_dv = state_hgrn_S.shape[3:]
    c_heads, c_dk, c_dv = state_gla_S.shape[3:]
    assert 1 + bs <= COND_ROWS and n_p % ts == 0

    x = jnp.concatenate([x_prompt.reshape(n_p, d), x_sample.reshape(n_s, d)], axis=0)
    cond = jnp.zeros((COND_ROWS, d), F32).at[0].set(c_ctx).at[1:1 + bs].set(c)
    mod = _modulation(cond, mod_w, mod_b).reshape(depth, COND_ROWS, N_MOD, d)
    cond_of = _CondRows(n_p, ts)
    parts = ((0, bp, tp, 1, True), (n_p, bs, ts, ts // GRID_W, False))

    lb_p = jax.nn.softmax(hgrn_lb_logits.astype(F32), axis=1)
    lb_all = jnp.cumsum(lb_p, axis=1) - lb_p[:, :1]

    d_ff = ffn_w_gate.shape[-1]
    w_gate = ffn_w_gate.astype(BF16).reshape(-1, d, d_ff)
    w_up = ffn_w_up.astype(BF16).reshape(-1, d, d_ff)
    w_down = ffn_w_down.astype(BF16).reshape(-1, d_ff, d)
    w_outs = (mlstm_w_out.astype(BF16), hgrn_w_out.astype(BF16), gla_w_out.astype(BF16))

    def swiglu(x, l, s, sub):
        h = _norm_mod(x, norm_g[l, sub], mod[l], sub, cond_of)
        a = _swiglu_up(h, w_gate, w_up, 2 * l + s)
        return _matmul_residual(a, w_down, 2 * l + s, x, mod[l], 3 * sub + 2, 0.5, cond_of)

    def mlstm(h, j):
        nq, nv = a_heads * a_dk, a_heads * a_dv
        w_in = mlstm_w_in[j]
        z = _matmul(h, w_in[:, :2 * nq + 2 * nv].astype(BF16), F32)
        n_tail = w_in.shape[1] - (2 * nq + 2 * nv)
        w_tail = jnp.pad(w_in[:, 2 * nq + 2 * nv:], ((0, 0), (0, 128 - n_tail))).astype(BF16)
        zt = _matmul(h, w_tail, F32, bn_pref=128)[:, :n_tail]
        gates = zt.reshape(-1, 2, 2, a_heads).transpose(1, 3, 0, 2)
        gate_b = mlstm_gate_b[j].astype(F32).transpose(0, 2, 1).reshape(2, a_heads, 1, 2)
        outs, finals = [], None
        for row0, batch, t_len, rows, is_ctx in parts:
            qk = _conv_silu(z, mlstm_conv[j], 2 * nq, row0, batch, t_len, rows)
            init = None if is_ctx else (state_mlstm_C[:, j], state_mlstm_n[:, j], state_mlstm_m[:, j])
            out2, fin = _mlstm_scan(qk, z, gates, gate_b, init, row0, batch, t_len, a_heads, a_dk, a_dv,
                                    max(g for g in (1, 2, 4) if a_heads % g == 0), is_ctx)
            outs.append(out2)
            finals = fin if is_ctx else finals
        return _head_norm_gate(outs[0], outs[1], z, 2 * nq + nv, mlstm_norm_g[j], "sigmoid"), finals

    def hgrn(h, j, l):
        nq, ni = b_heads * b_dk, b_heads * b_dv
        z = _matmul(h, hgrn_w_in[j].astype(BF16), F32)
        f_b = hgrn_f_b[j].astype(F32).reshape(2, 1, nq)
        lb = lb_all[:, l].reshape(2, 1, nq)
        nh = 2 if b_heads % 2 == 0 else 1
        ngrp = max(g for g in (1, 2, 4) if b_heads % (g * nh) == 0)
        outs, finals = [], None
        for row0, batch, t_len, rows, is_ctx in parts:
            qi = _conv_silu(z, hgrn_conv[j], nq + ni, row0, batch, t_len, rows)
            init = None if is_ctx else state_hgrn_S[:, j]
            out2, fin = _gated_scan("hgrn", (qi, z, f_b, lb), init, row0, batch, t_len,
                                    b_heads, b_dk, b_dv, ngrp, nh, is_ctx)
            outs.append(out2)
            finals = (fin,) if is_ctx else finals
        return _head_norm_gate(outs[0], outs[1], z, 3 * nq + ni, hgrn_norm_g[j], "silu"), finals

    def gla(h, j):
        nq, nv = c_heads * c_dk, c_heads * c_dv
        w_in = gla_w_in[j]
        z = _matmul(h, w_in[:, :2 * nq + 2 * nv].astype(BF16), F32)
        n_tail = w_in.shape[1] - (2 * nq + 2 * nv)
        w_tail = jnp.pad(w_in[:, 2 * nq + 2 * nv:], ((0, 0), (0, 128 - n_tail))).astype(BF16)
        zt = _matmul(h, w_tail, F32, bn_pref=128)[:, :n_tail]
        a_low = zt.reshape(-1, 2, n_tail // 2).transpose(1, 0, 2)
        b_a = gla_b_a[j].astype(F32).reshape(2, 1, nq)
        outs, finals = [], None
        for row0, batch, t_len, rows, is_ctx in parts:
            qk = _conv_silu(z, gla_conv[j], 2 * nq, row0, batch, t_len, rows)
            init = None if is_ctx else state_gla_S[:, j]
            out2, fin = _gated_scan("gla", (qk, z, a_low, gla_w_a2[j], b_a), init, row0, batch, t_len,
                                    c_heads, c_dk, c_dv, 1, 1, is_ctx)
            outs.append(out2)
            finals = (fin,) if is_ctx else finals
        return _head_norm_gate(outs[0], outs[1], z, 2 * nq + nv, gla_norm_g[j], "silu"), finals

    states = []
    for l in range(depth):
        x = swiglu(x, l, 0, 0)
        h = _norm_mod(x, norm_g[l, 1], mod[l], 1, cond_of)
        j, kind = l // 3, l % 3
        y, fin = mlstm(h, j) if kind == 0 else hgrn(h, j, l) if kind == 1 else gla(h, j)
        x = _matmul_residual(y, w_outs[kind], j, x, mod[l], 5, 1.0, cond_of, bm_pref=1024)
        x = swiglu(x, l, 1, 2)
        states.append(fin)

    y = _final_norm(x, final_norm_g)
    y_prompt = y[:n_p].reshape(bp, tp, d)
    y_sample = y[n_p:].reshape(bs, ts, d)
    new_c = jnp.stack([states[l][0] for l in range(0, depth, 3)], axis=1)
    new_n = jnp.stack([states[l][1] for l in range(0, depth, 3)], axis=1)
    new_m = jnp.stack([states[l][2] for l in range(0, depth, 3)], axis=1)
    new_hgrn = jnp.stack([states[l][0] for l in range(1, depth, 3)], axis=1)
    new_gla = jnp.stack([states[l][0] for l in range(2, depth, 3)], axis=1)
    return (y_prompt, y_sample, new_c, new_n, new_m, new_hgrn, new_gla)
```

```python
import functools
import math

import jax
import jax.numpy as jnp
import numpy as np
from jax import lax
from jax.experimental import pallas as pl
from jax.experimental.pallas import tpu as pltpu

F32 = jnp.float32
BF16 = jnp.bfloat16

GRID_W = 64
EPS = 1e-6
GLA_TAU = 16.0
MLSTM_CHUNK = 128
GATED_CHUNK = {"hgrn": 128, "gla": 256}
N_MOD = 9
COND_ROWS = 8

VMEM_LIMIT = 56 * 1024 * 1024
SCAN_TIME_BLOCK = 512
ROW_BLOCK = 256
ROW_TILE = 16


def _params(*sem):
    return pltpu.CompilerParams(dimension_semantics=sem, vmem_limit_bytes=VMEM_LIMIT)


def _pick(n, pref, align):
    if n <= pref:
        return n
    best = None
    for cand in range(align, pref + 1, align):
        if n % cand == 0:
            best = cand
    assert best is not None, (n, pref, align)
    return best


def _sigmoid(x):
    return 1.0 / (1.0 + jnp.exp(-x))


def _silu(x):
    return x * _sigmoid(x)


def _log_sigmoid(x):
    return jnp.minimum(x, 0.0) - jnp.log1p(jnp.exp(-jnp.abs(x)))


class _CondRows:
    def __init__(self, n_prompt_rows, t_sample):
        assert n_prompt_rows % t_sample == 0
        self.n_prompt_rows, self.t_sample = n_prompt_rows, t_sample

    def block_rows(self, pref):
        return _pick(self.t_sample, pref, 8)

    def index(self, bm):
        assert self.t_sample % bm == 0

        def f(i):
            start = i * bm
            return jnp.where(start < self.n_prompt_rows, 0, 1 + (start - self.n_prompt_rows) // self.t_sample)
        return f


def _mod_body(c_ref, w_ref, b_ref, o_ref):
    a = _silu(c_ref[...]).astype(BF16)
    o_ref[0] = jnp.dot(a, w_ref[0].astype(BF16), preferred_element_type=F32) + b_ref[0]


def _modulation(cond, mod_w, mod_b):
    depth, d, n = mod_w.shape
    bn = _pick(n, 512, 128)
    return pl.pallas_call(
        _mod_body,
        out_shape=jax.ShapeDtypeStruct((depth, COND_ROWS, n), F32),
        grid=(depth, n // bn),
        in_specs=[pl.BlockSpec((COND_ROWS, d), lambda l, j: (0, 0)),
                  pl.BlockSpec((1, d, bn), lambda l, j: (l, 0, j)),
                  pl.BlockSpec((1, 1, bn), lambda l, j: (l, 0, j))],
        out_specs=pl.BlockSpec((1, COND_ROWS, bn), lambda l, j: (l, 0, j)),
        compiler_params=_params("arbitrary", "arbitrary"),
        name="modulation",
    )(cond, mod_w, mod_b.reshape(depth, 1, n))


def _row_tiles(n_rows, tile, fn):
    tile = min(tile, n_rows)

    def step(r, carry):
        fn(pl.ds(pl.multiple_of(r * tile, tile), tile))
        return carry
    lax.fori_loop(0, n_rows // tile, step, 0, unroll=4)


def _norm_mod_body(sub, x_ref, g_ref, m_ref, o_ref):
    m = m_ref[0]
    gain = g_ref[...] * (1.0 + m[3 * sub + 1:3 * sub + 2, :])
    shift = m[3 * sub:3 * sub + 1, :]

    def tile(rows):
        x = x_ref[rows, :]
        y = x * lax.rsqrt(jnp.mean(x * x, axis=-1, keepdims=True) + EPS)
        o_ref[rows, :] = (y * gain + shift).astype(o_ref.dtype)
    _row_tiles(x_ref.shape[0], ROW_TILE, tile)


def _norm_mod(x, g, mod_l, sub, cond_of):
    n, d = x.shape
    bm = cond_of.block_rows(ROW_BLOCK)
    cr = cond_of.index(bm)
    return pl.pallas_call(
        functools.partial(_norm_mod_body, sub),
        out_shape=jax.ShapeDtypeStruct((n, d), BF16),
        grid=(n // bm,),
        in_specs=[pl.BlockSpec((bm, d), lambda i: (i, 0)),
                  pl.BlockSpec((1, d), lambda i: (0, 0)),
                  pl.BlockSpec((1, N_MOD, d), lambda i: (cr(i), 0, 0))],
        out_specs=pl.BlockSpec((bm, d), lambda i: (i, 0)),
        compiler_params=_params("parallel"),
        name="norm_mod",
    )(x, g.reshape(1, d), mod_l)


def _final_norm_body(x_ref, g_ref, o_ref):
    gain = g_ref[...]

    def tile(rows):
        x = x_ref[rows, :]
        o_ref[rows, :] = x * lax.rsqrt(jnp.mean(x * x, axis=-1, keepdims=True) + EPS) * gain
    _row_tiles(x_ref.shape[0], ROW_TILE, tile)


def _final_norm(x, g):
    n, d = x.shape
    bm = _pick(n, ROW_BLOCK, 8)
    return pl.pallas_call(
        _final_norm_body,
        out_shape=jax.ShapeDtypeStruct((n, d), F32),
        grid=(n // bm,),
        in_specs=[pl.BlockSpec((bm, d), lambda i: (i, 0)), pl.BlockSpec((1, d), lambda i: (0, 0))],
        out_specs=pl.BlockSpec((bm, d), lambda i: (i, 0)),
        compiler_params=_params("parallel"),
        name="final_norm",
    )(x, g.reshape(1, d))


def _matmul_body(x_ref, w_ref, o_ref):
    o_ref[...] = jnp.dot(x_ref[...], w_ref[...], preferred_element_type=F32).astype(o_ref.dtype)


def _matmul(x, w, out_dtype, bm_pref=1024, bn_pref=512):
    n, k = x.shape
    p = w.shape[1]
    bm = _pick(n, bm_pref, 8)
    bn = _pick(p, bn_pref, 128)
    return pl.pallas_call(
        _matmul_body,
        out_shape=jax.ShapeDtypeStruct((n, p), out_dtype),
        grid=(n // bm, p // bn),
        in_specs=[pl.BlockSpec((bm, k), lambda i, j: (i, 0)),
                  pl.BlockSpec((k, bn), lambda i, j: (0, j))],
        out_specs=pl.BlockSpec((bm, bn), lambda i, j: (i, j)),
        compiler_params=_params("parallel", "arbitrary"),
        name="matmul",
    )(x, w)


def _swiglu_up_body(h_ref, wg_ref, wu_ref, o_ref):
    h = h_ref[...]
    g = jnp.dot(h, wg_ref[...], preferred_element_type=F32)
    u = jnp.dot(h, wu_ref[...], preferred_element_type=F32)
    o_ref[...] = (_silu(g) * u).astype(o_ref.dtype)


def _swiglu_up(h, wg, wu, wi, bm_pref=1024, bn_pref=256):
    n, k = h.shape
    f = wg.shape[2]
    bm = _pick(n, bm_pref, 8)
    bn = _pick(f, bn_pref, 128)
    return pl.pallas_call(
        _swiglu_up_body,
        out_shape=jax.ShapeDtypeStruct((n, f), BF16),
        grid=(n // bm, f // bn),
        in_specs=[pl.BlockSpec((bm, k), lambda i, j: (i, 0)),
                  pl.BlockSpec((pl.Squeezed(), k, bn), lambda i, j: (wi, 0, j)),
                  pl.BlockSpec((pl.Squeezed(), k, bn), lambda i, j: (wi, 0, j))],
        out_specs=pl.BlockSpec((bm, bn), lambda i, j: (i, j)),
        compiler_params=_params("parallel", "arbitrary"),
        name="swiglu_up",
    )(h, wg, wu)


def _matmul_residual_body(gate_row, coef, a_ref, w_ref, x_ref, m_ref, o_ref):
    y = jnp.dot(a_ref[...], w_ref[...], preferred_element_type=F32)
    gate = m_ref[0][gate_row:gate_row + 1, :]
    o_ref[...] = x_ref[...] + (coef * gate) * y


def _matmul_residual(a, w, wi, x, mod_l, gate_row, coef, cond_of, bm_pref=512, bn_pref=512):
    n, k = a.shape
    d = w.shape[2]
    bm = cond_of.block_rows(bm_pref)
    bn = _pick(d, bn_pref, 128)
    cr = cond_of.index(bm)
    return pl.pallas_call(
        functools.partial(_matmul_residual_body, gate_row, coef),
        out_shape=jax.ShapeDtypeStruct((n, d), F32),
        grid=(n // bm, d // bn),
        in_specs=[pl.BlockSpec((bm, k), lambda i, j: (i, 0)),
                  pl.BlockSpec((pl.Squeezed(), k, bn), lambda i, j: (wi, 0, j)),
                  pl.BlockSpec((bm, bn), lambda i, j: (i, j)),
                  pl.BlockSpec((1, N_MOD, bn), lambda i, j: (cr(i), 0, j))],
        out_specs=pl.BlockSpec((bm, bn), lambda i, j: (i, j)),
        compiler_params=_params("parallel", "arbitrary"),
        name="matmul_residual",
    )(a, w, x, mod_l)


def _conv_body(rows, cols, z_ref, w_ref, o_ref):
    u = z_ref[...]
    t_len = u.shape[0]
    t = lax.broadcasted_iota(jnp.int32, (t_len, 1), 0)
    if rows == 1:
        r, c = jnp.zeros_like(t), t
    else:
        r, c = t >> (cols.bit_length() - 1), t & (cols - 1)
    w = w_ref[...]
    taps = (jnp.where(c >= 1, pltpu.roll(u, 1, 0), 0.0), u, jnp.where(c < cols - 1, pltpu.roll(u, t_len - 1, 0), 0.0))
    acc = None
    for i in range(3):
        if rows == 1 and i != 1:
            continue
        part = sum(taps[j] * w[3 * i + j:3 * i + j + 1, :] for j in range(3))
        if i != 1:
            part = jnp.where((r + (i - 1) >= 0) & (r + (i - 1) < rows),
                             pltpu.roll(part, (-(i - 1) * cols) % t_len, 0), 0.0)
        acc = part if acc is None else acc + part
    o_ref[...] = _silu(acc)


def _conv_silu(z, conv_w, n_conv, row0, batch, t_len, rows):
    cols = t_len // rows
    assert rows == 1 or (cols & (cols - 1)) == 0
    cb = _pick(n_conv, 256, 128)
    rb0 = row0 // t_len
    return pl.pallas_call(
        functools.partial(_conv_body, rows, cols),
        out_shape=jax.ShapeDtypeStruct((batch * t_len, n_conv), F32),
        grid=(batch, n_conv // cb),
        in_specs=[pl.BlockSpec((t_len, cb), lambda b, j: (rb0 + b, j)),
                  pl.BlockSpec((9, cb), lambda b, j: (0, j))],
        out_specs=pl.BlockSpec((t_len, cb), lambda b, j: (b, j)),
        compiler_params=_params("parallel", "arbitrary"),
        name="conv_silu",
    )(z, conv_w.reshape(9, n_conv))


def _time_block(d, tb, ntb):
    return tb + d * (ntb - 1 - 2 * tb)


def _dir_sign(d):
    return 1 - 2 * d


def _mlstm_body(has_init, want_final, ntb, hpb, dk, dv, *refs):
    it = iter(refs)
    q_ref, k_ref, v_ref, g_ref, gb_ref = (next(it) for _ in range(5))
    if has_init:
        c0_ref, n0_ref, m0_ref = (next(it) for _ in range(3))
    o_ref = next(it)
    if want_final:
        cf_ref, nf_ref, mf_ref = (next(it) for _ in range(3))
    c_s, n_s, m_s = (next(it) for _ in range(3))

    d = pl.program_id(2)
    tb = pl.program_id(3)
    L = MLSTM_CHUNK
    tbl = q_ref.shape[0]
    q_scale = dk ** -0.5

    @pl.when(tb == 0)
    def _():
        if has_init:
            c_s[...] = c0_ref[0, 0]
            n_s[...] = n0_ref[0, 0]
            m_s[...] = m0_ref[0, 0]
        else:
            c_s[...] = jnp.zeros_like(c_s)
            n_s[...] = jnp.zeros_like(n_s)
            m_s[...] = jnp.zeros_like(m_s)

    sign = _dir_sign(d)
    sub = lax.broadcasted_iota(jnp.int32, (L, L), 0)
    lane = lax.broadcasted_iota(jnp.int32, (L, L), 1)
    sees = sign * (sub - lane) >= 0
    sees_t = sign * (lane - sub) >= 0
    eye = sub == lane
    ncl = tbl // L

    def chunk(ci, carry):
        lc = ci + d * (ncl - 1 - 2 * ci)
        r0 = pl.multiple_of(lc * L, L)
        rows = pl.ds(r0, L)
        loaded = []
        for hh in range(hpb):
            ks = slice(hh * dk, (hh + 1) * dk)
            vs = slice(hh * dv, (hh + 1) * dv)
            loaded.append((q_ref[rows, ks], k_ref[rows, ks], v_ref[rows, vs], g_ref[0, hh, rows, :] + gb_ref[0, hh],
                           c_s[hh], n_s[hh], m_s[hh]))
        hs = range(hpb)
        q_raw, kc, v_raw, gates, c_st, n_st, m_st = zip(*loaded)
        qc = [(q_raw[h] * q_scale).astype(BF16) for h in hs]
        vc = [v_raw[h].astype(BF16) for h in hs]
        i_col = [gates[h][:, 0:1] for h in hs]
        f_col = [_log_sigmoid(gates[h][:, 1:2]) for h in hs]
        qk = [_dot_nt(qc[h], kc[h].astype(BF16)) for h in hs]
        qc_c = [jnp.dot(qc[h], c_st[h].astype(BF16), preferred_element_type=F32) for h in hs]
        b_row = [jnp.sum(jnp.where(sees_t, f_col[h], 0.0), axis=0, keepdims=True) for h in hs]
        i_row = [jnp.sum(jnp.where(eye, i_col[h], 0.0), axis=0, keepdims=True) for h in hs]
        b_end = [jnp.sum(f_col[h], axis=0, keepdims=True) for h in hs]
        b_col = [jnp.sum(jnp.where(eye, b_row[h], 0.0), axis=1, keepdims=True) for h in hs]
        dmat = [jnp.where(sees, b_col[h] - b_row[h] + i_row[h], -jnp.inf) for h in hs]
        w = [b_end[h] - b_col[h] + i_col[h] for h in hs]
        d_max = [jnp.max(dmat[h], axis=1, keepdims=True) for h in hs]
        w_max = [jnp.max(w[h], axis=0, keepdims=True) for h in hs]
        m_inter = [b_col[h] + m_st[h] for h in hs]
        m_t = [jnp.maximum(m_inter[h], d_max[h]) for h in hs]
        m_new = [jnp.maximum(b_end[h] + m_st[h], w_max[h]) for h in hs]
        s = [qk[h] * jnp.exp(dmat[h] - m_t[h]) for h in hs]
        e_inter = [jnp.exp(m_inter[h] - m_t[h]) for h in hs]
        kw = [kc[h] * jnp.exp(w[h] - m_new[h]) for h in hs]
        decay = [jnp.exp(b_end[h] + m_st[h] - m_new[h]) for h in hs]
        num = [jnp.dot(s[h].astype(BF16), vc[h], preferred_element_type=F32) + e_inter[h] * qc_c[h] for h in hs]
        kv = [_dot_tn(kw[h].astype(BF16), vc[h]) for h in hs]
        qn = [jnp.sum(qc[h].astype(F32) * n_st[h].astype(BF16).astype(F32), axis=1, keepdims=True) for h in hs]
        den = [jnp.sum(s[h], axis=1, keepdims=True) + e_inter[h] * qn[h] for h in hs]
        results = [(num[h] / jnp.maximum(jnp.abs(den[h]), jnp.exp(-m_t[h])), decay[h] * c_st[h] + kv[h],
                    decay[h] * n_st[h] + jnp.sum(kw[h], axis=0, keepdims=True), m_new[h]) for h in hs]
        for hh, (out, c_new, n_new, m_new) in enumerate(results):
            o_ref[0, rows, hh * dv:(hh + 1) * dv] = out
            c_s[hh] = c_new
            n_s[hh] = n_new
            m_s[hh] = m_new
        return carry

    lax.fori_loop(0, ncl, chunk, 0)

    if want_final:
        @pl.when(tb == ntb - 1)
        def _():
            cf_ref[0, 0] = c_s[...]
            nf_ref[0, 0] = n_s[...]
            mf_ref[0, 0] = m_s[...]


def _mlstm_scan(qk, z, gates, gate_b, init, row0, batch, t_len, heads, dk, dv, hpb, want_final):
    tbl = _pick(t_len, SCAN_TIME_BLOCK, MLSTM_CHUNK)
    ntb = t_len // tbl
    rb0 = row0 // tbl
    ng = heads // hpb
    v_col0 = (2 * heads * dk) // (hpb * dv)

    def part_rows(b, h, d, tb):
        return b * ntb + _time_block(d, tb, ntb)

    in_specs = [
        pl.BlockSpec((tbl, hpb * dk), lambda b, h, d, tb: (part_rows(b, h, d, tb), h)),
        pl.BlockSpec((tbl, hpb * dk), lambda b, h, d, tb: (part_rows(b, h, d, tb), ng + h)),
        pl.BlockSpec((tbl, hpb * dv), lambda b, h, d, tb: (rb0 + part_rows(b, h, d, tb), v_col0 + h)),
        pl.BlockSpec((1, hpb, tbl, 2), lambda b, h, d, tb: (d, h, rb0 + part_rows(b, h, d, tb), 0)),
        pl.BlockSpec((1, hpb, 1, 2), lambda b, h, d, tb: (d, h, 0, 0)),
    ]
    args = [qk, qk, z, gates, gate_b]
    state_specs = [
        pl.BlockSpec((1, 1, hpb, dk, dv), lambda b, h, d, tb: (b, d, h, 0, 0)),
        pl.BlockSpec((1, 1, hpb, 1, dk), lambda b, h, d, tb: (b, d, h, 0, 0)),
        pl.BlockSpec((1, 1, hpb, 1, 1), lambda b, h, d, tb: (b, d, h, 0, 0)),
    ]
    if init is not None:
        c0, n0, m0 = init
        in_specs += state_specs
        args += [c0, n0.reshape(batch, 2, heads, 1, dk), m0.reshape(batch, 2, heads, 1, 1)]
    out_shape = [jax.ShapeDtypeStruct((2, batch * t_len, heads * dv), F32)]
    out_specs = [pl.BlockSpec((1, tbl, hpb * dv), lambda b, h, d, tb: (d, part_rows(b, h, d, tb), h))]
    if want_final:
        out_shape += [jax.ShapeDtypeStruct((batch, 2, heads, dk, dv), F32),
                      jax.ShapeDtypeStruct((batch, 2, heads, 1, dk), F32),
                      jax.ShapeDtypeStruct((batch, 2, heads, 1, 1), F32)]
        out_specs += state_specs
    res = pl.pallas_call(
        functools.partial(_mlstm_body, init is not None, want_final, ntb, hpb, dk, dv),
        out_shape=out_shape,
        grid=(batch, ng, 2, ntb),
        in_specs=in_specs,
        out_specs=out_specs,
        scratch_shapes=[pltpu.VMEM((hpb, dk, dv), F32), pltpu.VMEM((hpb, 1, dk), F32),
                        pltpu.VMEM((hpb, 1, 1), F32)],
        compiler_params=_params("parallel", "parallel", "arbitrary", "arbitrary"),
        name="mlstm_scan",
    )(*args)
    if want_final:
        out, cf, nf, mf = res
        return out, (cf, nf.reshape(batch, 2, heads, dk), mf.reshape(batch, 2, heads))
    return res[0], None


def _pair_masks(L, nh):
    p = L.bit_length() - 1
    t = np.arange(L)[:, None]
    s = np.arange(L)[None, :]
    out = np.zeros((2, p + 1, L, L), np.float32)
    for j in range(p):
        same = (t >> (j + 1)) == (s >> (j + 1))
        t_hi, s_hi = ((t >> j) & 1) == 1, ((s >> j) & 1) == 1
        out[0, j] = same & t_hi & ~s_hi
        out[1, j] = same & ~t_hi & s_hi
    out[:, p] = np.eye(L, dtype=np.float32)
    return jnp.asarray(np.tile(out, (1, 1, 1, nh)).reshape(2, (p + 1) * L, nh * L))


def _block_rows(x, blk, r):
    n, w = x.shape
    if blk >= 16:
        return jnp.concatenate([jnp.broadcast_to(x[b0 + r:b0 + r + 1, :], (blk, w)) for b0 in range(0, n, blk)],
                               axis=0)
    x3 = x.reshape(n // 8, 8, w)
    sub = lax.broadcasted_iota(jnp.int32, (1, 8, 1), 1)
    out = None
    for s0 in range(0, 8, blk):
        bc = jnp.broadcast_to(x3[:, s0 + r:s0 + r + 1, :], x3.shape)
        out = bc if out is None else jnp.where(sub >= s0, bc, out)
    return out.reshape(n, w)


def _head_diag_rows(x, nh, w):
    if nh == 1:
        return x
    lane_head = lax.broadcasted_iota(jnp.int32, (1, nh * w), 1) // w
    return jnp.concatenate([jnp.where(lane_head == h, x, jnp.zeros_like(x)) for h in range(nh)], axis=0)


def _dot_nt(a, b):
    return lax.dot_general(a, b, (((1,), (1,)), ((), ())), preferred_element_type=F32)


def _dot_tn(a, b):
    return lax.dot_general(a, b, (((0,), (0,)), ((), ())), preferred_element_type=F32)


def _gated_body(kind, has_init, want_final, ntb, ngrp, nh, dk, dv, L, *refs):
    it = iter(refs)
    if kind == "hgrn":
        q_ref, v_ref, fpre_ref, fb_ref, lb_ref = (next(it) for _ in range(5))
    else:
        q_ref, k_ref, v_ref, alow_ref, wa_ref, ba_ref = (next(it) for _ in range(6))
    m_ref = next(it)
    if has_init:
        s0_ref = next(it)
    o_ref = next(it)
    if want_final:
        sf_ref = next(it)
    st_s, lg_s = next(it), next(it)
    if kind == "hgrn":
        kk_s = next(it)

    d = pl.program_id(2)
    tb = pl.program_id(3)
    tbl = q_ref.shape[0]
    p = L.bit_length() - 1
    kw, vw = nh * dk, nh * dv
    q_scale = 1.0 if kind == "hgrn" else dk ** -0.5

    @pl.when(tb == 0)
    def _():
        st_s[...] = jnp.zeros_like(st_s)
        if has_init:
            for gi in range(ngrp):
                for h in range(nh):
                    st_s[gi, h * dv:(h + 1) * dv, h * dk:(h + 1) * dk] = s0_ref[0, 0, gi * nh + h].T

    if kind == "hgrn":
        lb = lb_ref[0]
        f = lb + (1.0 - lb) * _sigmoid(fpre_ref[...] + fb_ref[0])
        lg_s[...] = jnp.log(f)
        kk_s[...] = 1.0 - f
    else:
        pre = jnp.dot(alow_ref[0].astype(BF16), wa_ref[0].astype(BF16), preferred_element_type=F32)
        lg_s[...] = _log_sigmoid(pre + ba_ref[0]) / GLA_TAU

    row = lax.broadcasted_iota(jnp.int32, (L, 1), 0)
    if nh > 1:
        own = ((lax.broadcasted_iota(jnp.int32, (vw, kw), 0) // dv)
               == (lax.broadcasted_iota(jnp.int32, (vw, kw), 1) // dk))
    ncl = tbl // L

    def run(rev):
        def chunk(ci, carry):
            r0 = pl.multiple_of(((ncl - 1 - ci) if rev else ci) * L, L)
            rows = pl.ds(r0, L)
            loaded = []
            for gi in range(ngrp):
                ks = slice(gi * kw, (gi + 1) * kw)
                vs = slice(gi * vw, (gi + 1) * vw)
                loaded.append((q_ref[rows, ks], kk_s[rows, ks] if kind == "hgrn" else k_ref[rows, ks],
                               v_ref[rows, vs], lg_s[rows, ks], st_s[gi]))
            results = []
            for q, k, v, seg, st in loaded:
                q = q * q_scale
                v = v.astype(BF16)
                a = m_ref[0, p * L:(p + 1) * L, :] * _dot_nt(q.astype(BF16), _head_diag_rows(k.astype(BF16), nh, dk))
                for j in range(p):
                    half = 1 << j
                    edge = _block_rows(seg, 2 * half, half if rev else half - 1)
                    later = (((row >> j) & 1) == 1) != rev
                    z = (jnp.exp(jnp.where(later, seg, edge - seg)) * jnp.where(later, q, k)).astype(BF16)
                    a = a + m_ref[0, j * L:(j + 1) * L, :] * _dot_nt(z, _head_diag_rows(z, nh, dk))
                    seg = seg + jnp.where(later, edge, 0.0)
                total = seg[0:1, :] if rev else seg[L - 1:L, :]
                out = (jnp.dot(a.astype(BF16), _head_diag_rows(v, nh, dv), preferred_element_type=F32)
                       + _dot_nt((q * jnp.exp(seg)).astype(BF16), st.astype(BF16)))
                st = st * jnp.exp(total) + _dot_tn(v, (k * jnp.exp(total - seg)).astype(BF16))
                results.append((out, st if nh == 1 else jnp.where(own, st, 0.0)))
            for gi, (out, st) in enumerate(results):
                o_ref[0, rows, gi * vw:(gi + 1) * vw] = out
                st_s[gi] = st
            return carry

        lax.fori_loop(0, ncl, chunk, 0)

    @pl.when(d == 0)
    def _():
        run(False)

    @pl.when(d == 1)
    def _():
        run(True)

    if want_final:
        @pl.when(tb == ntb - 1)
        def _():
            for gi in range(ngrp):
                for h in range(nh):
                    sf_ref[0, 0, gi * nh + h] = st_s[gi, h * dv:(h + 1) * dv, h * dk:(h + 1) * dk].T


def _gated_scan(kind, inputs, init, row0, batch, t_len, heads, dk, dv, ngrp, nh, want_final):
    chunk = min(GATED_CHUNK[kind], SCAN_TIME_BLOCK, t_len)
    tbl = _pick(t_len, SCAN_TIME_BLOCK, chunk)
    ntb = t_len // tbl
    rb0 = row0 // tbl
    hpb = ngrp * nh
    ng = heads // hpb
    kw, vw = hpb * dk, hpb * dv
    masks = _pair_masks(chunk, nh)
    mask_spec = pl.BlockSpec((1,) + masks.shape[1:], lambda b, g, d, tb: (d, 0, 0))

    def part_rows(b, g, d, tb):
        return b * ntb + _time_block(d, tb, ntb)

    def full_rows(b, g, d, tb):
        return rb0 + part_rows(b, g, d, tb)

    if kind == "hgrn":
        qi, z, f_b, lb = inputs
        f_col0 = (heads * dk + heads * dv) // kw
        in_specs = [
            pl.BlockSpec((tbl, kw), lambda b, g, d, tb: (part_rows(b, g, d, tb), g)),
            pl.BlockSpec((tbl, vw), lambda b, g, d, tb: (part_rows(b, g, d, tb), (heads * dk) // vw + g)),
            pl.BlockSpec((tbl, kw), lambda b, g, d, tb: (full_rows(b, g, d, tb), f_col0 + d * ng + g)),
            pl.BlockSpec((1, 1, kw), lambda b, g, d, tb: (d, 0, g)),
            pl.BlockSpec((1, 1, kw), lambda b, g, d, tb: (d, 0, g)),
        ]
        args = [qi, qi, z, f_b, lb]
    else:
        qk, z, a_low, w_a2, b_a = inputs
        rank = a_low.shape[-1]
        in_specs = [
            pl.BlockSpec((tbl, kw), lambda b, g, d, tb: (part_rows(b, g, d, tb), g)),
            pl.BlockSpec((tbl, kw), lambda b, g, d, tb: (part_rows(b, g, d, tb), ng + g)),
            pl.BlockSpec((tbl, vw), lambda b, g, d, tb: (full_rows(b, g, d, tb), (2 * heads * dk) // vw + g)),
            pl.BlockSpec((1, tbl, rank), lambda b, g, d, tb: (d, full_rows(b, g, d, tb), 0)),
            pl.BlockSpec((1, rank, kw), lambda b, g, d, tb: (d, 0, g)),
            pl.BlockSpec((1, 1, kw), lambda b, g, d, tb: (d, 0, g)),
        ]
        args = [qk, qk, z, a_low, w_a2, b_a]
    in_specs.append(mask_spec)
    args.append(masks)
    state_spec = pl.BlockSpec((1, 1, hpb, dk, dv), lambda b, g, d, tb: (b, d, g, 0, 0))
    if init is not None:
        in_specs.append(state_spec)
        args.append(init)
    out_shape = [jax.ShapeDtypeStruct((2, batch * t_len, heads * dv), F32)]
    out_specs = [pl.BlockSpec((1, tbl, vw), lambda b, g, d, tb: (d, part_rows(b, g, d, tb), g))]
    if want_final:
        out_shape.append(jax.ShapeDtypeStruct((batch, 2, heads, dk, dv), F32))
        out_specs.append(state_spec)
    scratch = [pltpu.VMEM((ngrp, nh * dv, nh * dk), F32), pltpu.VMEM((tbl, kw), F32)]
    if kind == "hgrn":
        scratch.append(pltpu.VMEM((tbl, kw), F32))
    res = pl.pallas_call(
        functools.partial(_gated_body, kind, init is not None, want_final, ntb, ngrp, nh, dk, dv, chunk),
        out_shape=out_shape,
        grid=(batch, ng, 2, ntb),
        in_specs=in_specs,
        out_specs=out_specs,
        scratch_shapes=scratch,
        compiler_params=_params("parallel", "parallel", "arbitrary", "arbitrary"),
        name=kind + "_scan",
    )(*args)
    return (res[0], res[1]) if want_final else (res[0], None)


def _head_norm_body(gate_kind, heads, dv, n_ctx_blocks, oc_ref, ol_ref, z_ref, g_ref, y_ref):
    def emit(o_ref):
        def tile(rows):
            for h in range(heads):
                sl = slice(h * dv, (h + 1) * dv)
                o = o_ref[0, rows, sl] + o_ref[1, rows, sl]
                o = o * lax.rsqrt(jnp.mean(o * o, axis=-1, keepdims=True) + EPS) * g_ref[:, sl]
                zg = z_ref[rows, sl]
                gate = _sigmoid(zg) if gate_kind == "sigmoid" else _silu(zg)
                y_ref[rows, sl] = (gate * o).astype(y_ref.dtype)
        _row_tiles(y_ref.shape[0], ROW_TILE, tile)

    @pl.when(pl.program_id(0) < n_ctx_blocks)
    def _():
        emit(oc_ref)

    @pl.when(pl.program_id(0) >= n_ctx_blocks)
    def _():
        emit(ol_ref)


def _head_norm_gate(out_ctx, out_lat, z, gate_col0, norm_g, gate_kind):
    _, n_c, hv = out_ctx.shape
    n_l = out_lat.shape[1]
    heads, dv = norm_g.shape
    bm = _pick(math.gcd(n_c, n_l), ROW_BLOCK, 8)
    ncb = n_c // bm
    return pl.pallas_call(
        functools.partial(_head_norm_body, gate_kind, heads, dv, ncb),
        out_shape=jax.ShapeDtypeStruct((n_c + n_l, hv), BF16),
        grid=((n_c + n_l) // bm,),
        in_specs=[pl.BlockSpec((2, bm, hv), lambda i: (0, jnp.minimum(i, ncb - 1), 0)),
                  pl.BlockSpec((2, bm, hv), lambda i: (0, jnp.maximum(i - ncb, 0), 0)),
                  pl.BlockSpec((bm, hv), lambda i: (i, gate_col0 // hv)),
                  pl.BlockSpec((1, hv), lambda i: (0, 0))],
        out_specs=pl.BlockSpec((bm, hv), lambda i: (i, 0)),
        compiler_params=_params("arbitrary"),
        name="head_norm_gate",
    )(out_ctx, out_lat, z, norm_g.reshape(1, hv))


def kernel(x_prompt, x_sample, c, c_ctx, state_mlstm_C, state_mlstm_n, state_mlstm_m, state_hgrn_S,
           state_gla_S, mod_w, mod_b, norm_g, ffn_w_gate, ffn_w_up, ffn_w_down, mlstm_w_in, mlstm_conv,
           mlstm_gate_b, mlstm_norm_g, mlstm_w_out, hgrn_w_in, hgrn_conv, hgrn_f_b, hgrn_lb_logits,
           hgrn_norm_g, hgrn_w_out, gla_w_in, gla_conv, gla_w_a2, gla_b_a, gla_norm_g, gla_w_out,
           final_norm_g):
    bp, tp, d = x_prompt.shape
    bs, ts, _ = x_sample.shape
    n_p, n_s = bp * tp, bs * ts
    depth = mod_w.shape[0]
    a_heads, a_dk, a_dv = state_mlstm_C.shape[3:]
    b_heads, b_dk, b_dv = state_hgrn_S.shape[3:]
    c_heads, c_dk, c_dv = state_gla_S.shape[3:]
    assert 1 + bs <= COND_ROWS and n_p % ts == 0

    x = jnp.concatenate([x_prompt.reshape(n_p, d), x_sample.reshape(n_s, d)], axis=0)
    cond = jnp.zeros((COND_ROWS, d), F32).at[0].set(c_ctx).at[1:1 + bs].set(c)
    mod = _modulation(cond, mod_w, mod_b).reshape(depth, COND_ROWS, N_MOD, d)
    cond_of = _CondRows(n_p, ts)
    parts = ((0, bp, tp, 1, True), (n_p, bs, ts, ts // GRID_W, False))

    lb_p = jax.nn.softmax(hgrn_lb_logits.astype(F32), axis=1)
    lb_all = jnp.cumsum(lb_p, axis=1) - lb_p[:, :1]

    d_ff = ffn_w_gate.shape[-1]
    w_gate = ffn_w_gate.astype(BF16).reshape(-1, d, d_ff)
    w_up = ffn_w_up.astype(BF16).reshape(-1, d, d_ff)
    w_down = ffn_w_down.astype(BF16).reshape(-1, d_ff, d)
    w_outs = (mlstm_w_out.astype(BF16), hgrn_w_out.astype(BF16), gla_w_out.astype(BF16))

    def swiglu(x, l, s, sub):
        h = _norm_mod(x, norm_g[l, sub], mod[l], sub, cond_of)
        a = _swiglu_up(h, w_gate, w_up, 2 * l + s)
        return _matmul_residual(a, w_down, 2 * l + s, x, mod[l], 3 * sub + 2, 0.5, cond_of)

    def mlstm(h, j):
        nq, nv = a_heads * a_dk, a_heads * a_dv
        w_in = mlstm_w_in[j]
        z = _matmul(h, w_in[:, :2 * nq + 2 * nv].astype(BF16), F32)
        n_tail = w_in.shape[1] - (2 * nq + 2 * nv)
        w_tail = jnp.pad(w_in[:, 2 * nq + 2 * nv:], ((0, 0), (0, 128 - n_tail))).astype(BF16)
        zt = _matmul(h, w_tail, F32, bn_pref=128)[:, :n_tail]
        gates = zt.reshape(-1, 2, 2, a_heads).transpose(1, 3, 0, 2)
        gate_b = mlstm_gate_b[j].astype(F32).transpose(0, 2, 1).reshape(2, a_heads, 1, 2)
        outs, finals = [], None
        for row0, batch, t_len, rows, is_ctx in parts:
            qk = _conv_silu(z, mlstm_conv[j], 2 * nq, row0, batch, t_len, rows)
            init = None if is_ctx else (state_mlstm_C[:, j], state_mlstm_n[:, j], state_mlstm_m[:, j])
            out2, fin = _mlstm_scan(qk, z, gates, gate_b, init, row0, batch, t_len, a_heads, a_dk, a_dv,
                                    max(g for g in (1, 2, 4) if a_heads % g == 0), is_ctx)
            outs.append(out2)
            finals = fin if is_ctx else finals
        return _head_norm_gate(outs[0], outs[1], z, 2 * nq + nv, mlstm_norm_g[j], "sigmoid"), finals

    def hgrn(h, j, l):
        nq, ni = b_heads * b_dk, b_heads * b_dv
        z = _matmul(h, hgrn_w_in[j].astype(BF16), F32)
        f_b = hgrn_f_b[j].astype(F32).reshape(2, 1, nq)
        lb = lb_all[:, l].reshape(2, 1, nq)
        nh = 2 if b_heads % 2 == 0 else 1
        ngrp = max(g for g in (1, 2, 4) if b_heads % (g * nh) == 0)
        outs, finals = [], None
        for row0, batch, t_len, rows, is_ctx in parts:
            qi = _conv_silu(z, hgrn_conv[j], nq + ni, row0, batch, t_len, rows)
            init = None if is_ctx else state_hgrn_S[:, j]
            out2, fin = _gated_scan("hgrn", (qi, z, f_b, lb), init, row0, batch, t_len,
                                    b_heads, b_dk, b_dv, ngrp, nh, is_ctx)
            outs.append(out2)
            finals = (fin,) if is_ctx else finals
        return _head_norm_gate(outs[0], outs[1], z, 3 * nq + ni, hgrn_norm_g[j], "silu"), finals

    def gla(h, j):
        nq, nv = c_heads * c_dk, c_heads * c_dv
        w_in = gla_w_in[j]
        z = _matmul(h, w_in[:, :2 * nq + 2 * nv].astype(BF16), F32)
        n_tail = w_in.shape[1] - (2 * nq + 2 * nv)
        w_tail = jnp.pad(w_in[:, 2 * nq + 2 * nv:], ((0, 0), (0, 128 - n_tail))).astype(BF16)
        zt = _matmul(h, w_tail, F32, bn_pref=128)[:, :n_tail]
        a_low = zt.reshape(-1, 2, n_tail // 2).transpose(1, 0, 2)
        b_a = gla_b_a[j].astype(F32).reshape(2, 1, nq)
        outs, finals = [], None
        for row0, batch, t_len, rows, is_ctx in parts:
            qk = _conv_silu(z, gla_conv[j], 2 * nq, row0, batch, t_len, rows)
            init = None if is_ctx else state_gla_S[:, j]
            out2, fin = _gated_scan("gla", (qk, z, a_low, gla_w_a2[j], b_a), init, row0, batch, t_len,
                                    c_heads, c_dk, c_dv, 1, 1, is_ctx)
            outs.append(out2)
            finals = (fin,) if is_ctx else finals
        return _head_norm_gate(outs[0], outs[1], z, 2 * nq + nv, gla_norm_g[j], "silu"), finals

    states = []
    for l in range(depth):
        x = swiglu(x, l, 0, 0)
        h = _norm_mod(x, norm_g[l, 1], mod[l], 1, cond_of)
        j, kind = l // 3, l % 3
        y, fin = mlstm(h, j) if kind == 0 else hgrn(h, j, l) if kind == 1 else gla(h, j)
        x = _matmul_residual(y, w_outs[kind], j, x, mod[l], 5, 1.0, cond_of, bm_pref=1024)
        x = swiglu(x, l, 1, 2)
        states.append(fin)

    y = _final_norm(x, final_norm_g)
    y_prompt = y[:n_p].reshape(bp, tp, d)
    y_sample = y[n_p:].reshape(bs, ts, d)
    new_c = jnp.stack([states[l][0] for l in range(0, depth, 3)], axis=1)
    new_n = jnp.stack([states[l][1] for l in range(0, depth, 3)], axis=1)
    new_m = jnp.stack([states[l][2] for l in range(0, depth, 3)], axis=1)
    new_hgrn = jnp.stack([states[l][0] for l in range(1, depth, 3)], axis=1)
    new_gla = jnp.stack([states[l][0] for l in range(2, depth, 3)], axis=1)
    return (y_prompt, y_sample, new_c, new_n, new_m, new_hgrn, new_gla)
```
